```python
import math, functools
import jax, jax.numpy as jnp
from jax import lax
import numpy as np

D_MODEL = 2048
BATCH = 4
SEQ = 4096
DEPTH = 2

DN_HEADS = 8
DN_HEAD_DIM = 128
DN_WIDTH = DN_HEADS * DN_HEAD_DIM
DN_CONV = 4
DN_CHUNK = 64
SG_GROUPS = 8
SG_GROUP_DIM = 128
SG_WIDTH = SG_GROUPS * SG_GROUP_DIM
SG_CHUNK = 128
D_FF = 5632
FFN_CONV = 3
EPS = 1e-6

SPLIT_SIZES = (3 * DN_WIDTH, DN_WIDTH, DN_HEADS, DN_HEADS, SG_WIDTH, SG_WIDTH, D_MODEL, D_MODEL)
SPLIT_POINTS = tuple(int(s) for s in np.cumsum(SPLIT_SIZES)[:-1])
IN_COLS = int(sum(SPLIT_SIZES))

kernel_name = "hybrid_deltanet_gmlp_convffn_gated_merge"


def rmsnorm(x, g):
    xf = x.astype(jnp.float32)
    y = xf * lax.rsqrt(jnp.mean(xf * xf, axis=-1, keepdims=True) + EPS)
    return (y * g.astype(jnp.float32)).astype(x.dtype)


def layernorm(x, g, b):
    xf = x.astype(jnp.float32)
    mu = jnp.mean(xf, axis=-1, keepdims=True)
    xc = xf - mu
    y = xc * lax.rsqrt(jnp.mean(xc * xc, axis=-1, keepdims=True) + EPS)
    return (y * g.astype(jnp.float32) + b.astype(jnp.float32)).astype(x.dtype)


def l2norm(x):
    xf = x.astype(jnp.float32)
    return xf * lax.rsqrt(jnp.sum(xf * xf, axis=-1, keepdims=True) + EPS)


def causal_dwconv(x, w):
    K = w.shape[0]
    T = x.shape[1]
    xp = jnp.pad(x, ((0, 0), (K - 1, 0), (0, 0)))
    out = xp[:, 0:T] * w[0]
    for j in range(1, K):
        out = out + xp[:, j:j + T] * w[j]
    return out


def gated_delta_rule_chunked(q, k, v, g, beta):
    B, T, H, Dk = q.shape
    Dv = v.shape[-1]
    C = DN_CHUNK
    N = T // C

    def to_chunks(t):
        t = t.astype(jnp.float32).reshape((B, N, C, H) + t.shape[3:])
        return jnp.moveaxis(t, 3, 1)

    q = to_chunks(q) * (Dk ** -0.5)
    k = to_chunks(k)
    v = to_chunks(v)
    g = jnp.cumsum(to_chunks(g), axis=-1)
    beta = to_chunks(beta)
    k_beta = k * beta[..., None]
    v_beta = v * beta[..., None]

    causal = jnp.tril(jnp.ones((C, C), dtype=bool))
    strict = jnp.tril(jnp.ones((C, C), dtype=bool), -1)
    decay = jnp.exp(jnp.where(causal, g[..., :, None] - g[..., None, :], -jnp.inf))

    L = jnp.where(strict, jnp.einsum('bhnid,bhnjd->bhnij', k_beta, k) * decay, 0.0)
    eye = jnp.eye(C, dtype=jnp.float32)
    Tinv = lax.linalg.triangular_solve(L + eye, jnp.broadcast_to(eye, L.shape),
                                       left_side=True, lower=True, unit_diagonal=True)
    u = jnp.einsum('bhnij,bhnjv->bhniv', Tinv, v_beta)
    w = jnp.einsum('bhnij,bhnjk->bhnik', Tinv, k_beta * jnp.exp(g)[..., None])

    attn = jnp.where(causal, jnp.einsum('bhnid,bhnjd->bhnij', q, k) * decay, 0.0)
    q_dec = q * jnp.exp(g)[..., None]
    g_last = g[..., -1]
    k_dec = k * jnp.exp(g_last[..., None] - g)[..., None]

    xs = tuple(jnp.moveaxis(t, 2, 0) for t in (u, w, attn, q_dec, k_dec, g_last))

    def step(S, inp):
        u_n, w_n, a_n, qd_n, kd_n, gl_n = inp
        v_new = u_n - jnp.einsum('bhck,bhkv->bhcv', w_n, S)
        o_n = (jnp.einsum('bhck,bhkv->bhcv', qd_n, S)
               + jnp.einsum('bhij,bhjv->bhiv', a_n, v_new))
        S = S * jnp.exp(gl_n)[..., None, None] + jnp.einsum('bhck,bhcv->bhkv', kd_n, v_new)
        return S, o_n

    S0 = jnp.zeros((B, H, Dk, Dv), jnp.float32)
    _, o = lax.scan(step, S0, xs)
    return jnp.transpose(o, (1, 0, 3, 2, 4)).reshape(B, T, H, Dv)


def setup_inputs(seed: int = 0) -> dict:
    key = jax.random.key(seed)
    ks = jax.random.split(key, 24)
    f32 = jnp.float32

    def nrm(k, shape, scale):
        return jax.random.normal(k, shape, f32) * scale

    x = jax.random.normal(ks[0], (BATCH, SEQ, D_MODEL), f32)
    norm1_g = 1.0 + nrm(ks[1], (DEPTH, D_MODEL), 0.02)
    w_in = nrm(ks[2], (DEPTH, D_MODEL, IN_COLS), D_MODEL ** -0.5)
    dn_conv_w = nrm(ks[3], (DEPTH, DN_CONV, 3 * DN_WIDTH), DN_CONV ** -0.5)
    dn_a_log = jnp.log(jax.random.uniform(ks[4], (DEPTH, DN_HEADS), f32, 1.0, 16.0))
    dt = jnp.exp(jax.random.uniform(ks[5], (DEPTH, DN_HEADS), f32,
                                    math.log(0.001), math.log(0.1)))
    dn_dt_bias = dt + jnp.log(-jnp.expm1(-dt))
    dn_onorm_g = 1.0 + nrm(ks[6], (DEPTH, DN_HEAD_DIM), 0.02)
    sg_ln_g = 1.0 + nrm(ks[7], (DEPTH, SG_WIDTH), 0.02)
    sg_ln_b = nrm(ks[8], (DEPTH, SG_WIDTH), 0.02)
    sg_w = nrm(ks[9], (DEPTH, SG_GROUPS, SG_CHUNK, SG_CHUNK), 0.5 * SG_CHUNK ** -0.5)
    sg_b = 1.0 + nrm(ks[10], (DEPTH, SG_GROUPS, SG_CHUNK), 0.02)
    w_branch_a = nrm(ks[11], (DEPTH, DN_WIDTH, D_MODEL), DN_WIDTH ** -0.5)
    w_branch_b = nrm(ks[12], (DEPTH, SG_WIDTH, D_MODEL), SG_WIDTH ** -0.5)
    w_out = nrm(ks[13], (DEPTH, D_MODEL, D_MODEL), D_MODEL ** -0.5)
    norm2_g = 1.0 + nrm(ks[14], (DEPTH, D_MODEL), 0.02)
    ffn_w_gate = nrm(ks[15], (DEPTH, D_MODEL, D_FF), D_MODEL ** -0.5)
    ffn_w_up = nrm(ks[16], (DEPTH, D_MODEL, D_FF), D_MODEL ** -0.5)
    ffn_conv_w = nrm(ks[17], (DEPTH, FFN_CONV, D_FF), FFN_CONV ** -0.5)
    ffn_conv_b = nrm(ks[18], (DEPTH, D_FF), 0.02)
    ffn_w_down = nrm(ks[19], (DEPTH, D_FF, D_MODEL), D_FF ** -0.5)
    final_norm_g = 1.0 + nrm(ks[20], (D_MODEL,), 0.02)
    return {"x": x, "norm1_g": norm1_g, "w_in": w_in, "dn_conv_w": dn_conv_w,
            "dn_a_log": dn_a_log, "dn_dt_bias": dn_dt_bias, "dn_onorm_g": dn_onorm_g,
            "sg_ln_g": sg_ln_g, "sg_ln_b": sg_ln_b, "sg_w": sg_w, "sg_b": sg_b,
            "w_branch_a": w_branch_a, "w_branch_b": w_branch_b, "w_out": w_out,
            "norm2_g": norm2_g, "ffn_w_gate": ffn_w_gate, "ffn_w_up": ffn_w_up,
            "ffn_conv_w": ffn_conv_w, "ffn_conv_b": ffn_conv_b, "ffn_w_down": ffn_w_down,
            "final_norm_g": final_norm_g}


def reference(x, norm1_g, w_in, dn_conv_w, dn_a_log, dn_dt_bias, dn_onorm_g,
              sg_ln_g, sg_ln_b, sg_w, sg_b, w_branch_a, w_branch_b, w_out,
              norm2_g, ffn_w_gate, ffn_w_up, ffn_conv_w, ffn_conv_b, ffn_w_down,
              final_norm_g):
    B, T, _ = x.shape
    sg_mask = jnp.tril(jnp.ones((SG_CHUNK, SG_CHUNK), dtype=bool))
    for l in range(DEPTH):
        h = rmsnorm(x, norm1_g[l])
        proj = h @ w_in[l]
        qkv, z, b_raw, a_raw, u_raw, v_raw, ga_raw, gb_raw = jnp.split(proj, SPLIT_POINTS, axis=-1)

        qkv = jax.nn.silu(causal_dwconv(qkv, dn_conv_w[l]))
        q, k, v = jnp.split(qkv, 3, axis=-1)
        q = l2norm(q.reshape(B, T, DN_HEADS, DN_HEAD_DIM))
        k = l2norm(k.reshape(B, T, DN_HEADS, DN_HEAD_DIM))
        v = v.reshape(B, T, DN_HEADS, DN_HEAD_DIM)
        beta = jax.nn.sigmoid(b_raw.astype(jnp.float32))
        g = -jnp.exp(dn_a_log[l].astype(jnp.float32)) * jax.nn.softplus(
            a_raw.astype(jnp.float32) + dn_dt_bias[l].astype(jnp.float32))
        o = gated_delta_rule_chunked(q, k, v, g, beta)
        o = rmsnorm(o, dn_onorm_g[l]) * jax.nn.silu(
            z.reshape(B, T, DN_HEADS, DN_HEAD_DIM).astype(jnp.float32))
        y_a = o.reshape(B, T, DN_WIDTH).astype(x.dtype)

        u = jax.nn.gelu(u_raw, approximate=False)
        vg = layernorm(jax.nn.gelu(v_raw, approximate=False), sg_ln_g[l], sg_ln_b[l])
        vg = vg.reshape(B, T // SG_CHUNK, SG_CHUNK, SG_GROUPS, SG_GROUP_DIM)
        ws = jnp.where(sg_mask, sg_w[l], 0.0)
        mixed = (jnp.einsum('gij,bnjgc->bnigc', ws, vg)
                 + jnp.transpose(sg_b[l])[None, None, :, :, None])
        y_b = u * mixed.reshape(B, T, SG_WIDTH)

        merged = (jax.nn.sigmoid(ga_raw) * (y_a @ w_branch_a[l])
                  + jax.nn.sigmoid(gb_raw) * (y_b @ w_branch_b[l]))
        x = x + merged @ w_out[l]

        h2 = rmsnorm(x, norm2_g[l])
        gate = causal_dwconv(h2 @ ffn_w_gate[l], ffn_conv_w[l]) + ffn_conv_b[l]
        x = x + (jax.nn.silu(gate) * (h2 @ ffn_w_up[l])) @ ffn_w_down[l]
    return rmsnorm(x, final_norm_g)
```

```python
import functools

import jax
import jax.numpy as jnp
from jax import lax
from jax.experimental import pallas as pl
from jax.experimental.pallas import tpu as pltpu

EPS = 1e-6
LANES = 128
DN_CHUNK = 64
SG_CHUNK = 128
HALO = 8
VMEM_LIMIT = 56 * 1024 * 1024

F32 = jnp.float32
BF16 = jnp.bfloat16


def _dot(a, b):
    return jnp.dot(a, b, preferred_element_type=F32)


def _dot_nt(a, b):
    return lax.dot_general(a, b, (((1,), (1,)), ((), ())), preferred_element_type=F32)


def _dot_tn(a, b):
    return lax.dot_general(a, b, (((0,), (0,)), ((), ())), preferred_element_type=F32)


def _split(a):
    hi = a.astype(BF16)
    lo = (a - hi.astype(F32)).astype(BF16)
    return hi, lo


def _dot_split(a_hl, b_hl):
    ah, al = a_hl
    bh, bl = b_hl
    return _dot(ah, bh) + (_dot(ah, bl) + _dot(al, bh))


def _split3(a):
    hi = a.astype(BF16)
    r = a - hi.astype(F32)
    mid = r.astype(BF16)
    lo = (r - mid.astype(F32)).astype(BF16)
    return hi, mid, lo


def _rmsnorm_rows(x_ref, g_ref, o_ref, rows):
    n = x_ref.shape[0] // rows

    def body(i, _):
        r = pl.ds(pl.multiple_of(i * rows, rows), rows)
        x = x_ref[r, :]
        ms = jnp.mean(x * x, axis=-1, keepdims=True)
        o_ref[r, :] = (x * lax.rsqrt(ms + EPS) * g_ref[...]).astype(o_ref.dtype)
        return 0

    lax.fori_loop(0, n, body, 0)


def _inproj_kernel(x_ref, g_ref, w_ref, wba_ref, o_ref, ba_ref, h_ref):
    @pl.when(pl.program_id(1) == 0)
    def _():
        _rmsnorm_rows(x_ref, g_ref, h_ref, min(256, x_ref.shape[0]))
        ba_ref[...] = _dot(h_ref[...], wba_ref[...])

    o_ref[...] = _dot(h_ref[...], w_ref[...])


def _inproj(x, g, w, wba, *, tm, tn):
    n, d = x.shape
    nc = w.shape[1]
    return pl.pallas_call(
        _inproj_kernel,
        grid=(n // tm, nc // tn),
        in_specs=[
            pl.BlockSpec((tm, d), lambda i, j: (i, 0)),
            pl.BlockSpec((1, d), lambda i, j: (0, 0)),
            pl.BlockSpec((d, tn), lambda i, j: (0, j)),
            pl.BlockSpec((d, LANES), lambda i, j: (0, 0)),
        ],
        out_specs=[
            pl.BlockSpec((tm, tn), lambda i, j: (i, j)),
            pl.BlockSpec((tm, LANES), lambda i, j: (i, 0)),
        ],
        out_shape=[
            jax.ShapeDtypeStruct((n, nc), F32),
            jax.ShapeDtypeStruct((n, LANES), F32),
        ],
        scratch_shapes=[pltpu.VMEM((tm, d), BF16)],
        compiler_params=pltpu.CompilerParams(
            dimension_semantics=("arbitrary", "arbitrary"),
            vmem_limit_bytes=VMEM_LIMIT),
        name="inproj",
    )(x, g, w, wba)


def _deltanet_kernel(qkv_ref, z_ref, ba_ref, cw_ref, adt_ref, og_ref, y_ref,
                     xe_ref, qkv_s, gb_s, s_ref, *, heads, tb):
    C = DN_CHUNK
    kconv = cw_ref.shape[0]

    @pl.when(pl.program_id(1) == 0)
    def _():
        xe_ref[0:HALO, :] = jnp.zeros((HALO, xe_ref.shape[1]), F32)
        s_ref[...] = jnp.zeros(s_ref.shape, F32)

    xe_ref[HALO:HALO + tb, :] = qkv_ref[...]
    scale = float(LANES) ** -0.5
    for c in range(3 * heads):
        cols = slice(c * LANES, (c + 1) * LANES)
        acc = None
        for j in range(kconv):
            off = HALO - (kconv - 1) + j
            term = xe_ref[off:off + tb, cols] * cw_ref[j:j + 1, cols]
            acc = term if acc is None else acc + term
        y = acc * jax.nn.sigmoid(acc)
        if c < 2 * heads:
            y = y * lax.rsqrt(jnp.sum(y * y, axis=-1, keepdims=True) + EPS)
            if c < heads:
                y = y * scale
        qkv_s[:, cols] = y
    xe_ref[0:HALO, :] = xe_ref[tb:tb + HALO, :]

    ba = ba_ref[...]
    beta_all = jax.nn.sigmoid(ba)
    xa = ba + adt_ref[1:2, :]
    softplus = jnp.maximum(xa, 0.0) + jnp.log1p(jnp.exp(-jnp.abs(xa)))
    g_all = -jnp.exp(adt_ref[0:1, :]) * softplus
    lane = lax.broadcasted_iota(jnp.int32, (tb, LANES), 1)
    for h in range(heads):
        bcol = jnp.sum(jnp.where(lane == h, beta_all, 0.0), axis=-1, keepdims=True)
        gcol = jnp.sum(jnp.where(lane == heads + h, g_all, 0.0), axis=-1, keepdims=True)
        gb_s[h] = jnp.broadcast_to(bcol, (tb, LANES))
        gb_s[heads + h] = jnp.broadcast_to(gcol, (tb, LANES))

    row = lax.broadcasted_iota(jnp.int32, (C, C), 0)
    col = lax.broadcasted_iota(jnp.int32, (C, C), 1)
    incl = row >= col
    strict = row > col
    eye = jnp.where(row == col, 1.0, 0.0).astype(F32)
    tril16 = jnp.where(incl, 1.0, 0.0).astype(BF16)
    ones16 = jnp.ones((C, C), BF16)
    upper_incl = row <= col
    sizes = []
    s = 1
    while s < C:
        sizes.append(s)
        s *= 2
    level_masks = [
        ((row // (2 * s)) == (col // (2 * s))) & ((row & s) != 0) & ((col & s) == 0)
        for s in sizes
    ]

    def chunk_body(c, carry):
        r = pl.ds(pl.multiple_of(c * C, C), C)
        for h in range(heads):
            q = qkv_s[r, h * LANES:(h + 1) * LANES]
            k = qkv_s[r, (heads + h) * LANES:(heads + h + 1) * LANES]
            v = qkv_s[r, (2 * heads + h) * LANES:(2 * heads + h + 1) * LANES]
            beta = gb_s[h, r, :]
            g_b = gb_s[heads + h, r, :]

            g3 = _split3(g_b)
            gcol = _dot(tril16, g3[0]) + (_dot(tril16, g3[1]) + _dot(tril16, g3[2]))
            gm3 = _split3(jnp.where(upper_incl, g_b[:, :C], 0.0))
            grow = _dot(ones16, gm3[0]) + (_dot(ones16, gm3[1]) + _dot(ones16, gm3[2]))
            decay = jnp.where(incl, jnp.exp(gcol[:, :C] - grow), 0.0)
            egc = jnp.exp(gcol)
            g_last = gcol[C - 1:C, :]

            kb = k * beta
            vb = v * beta
            k16 = k.astype(BF16)
            lmat = jnp.where(strict, _dot_nt(kb.astype(BF16), k16) * decay, 0.0)
            attn = _dot_nt(q.astype(BF16), k16) * decay

            tinv = eye - jnp.where(level_masks[0], lmat, 0.0)
            for m in level_masks[1:]:
                t_hl = _split(tinv)
                c_hl = _split(jnp.where(m, lmat, 0.0))
                tc = _dot_split(t_hl, c_hl)
                tinv = tinv - _dot_split(_split(tc), t_hl)

            rhs = jnp.concatenate([vb, kb * egc], axis=1).astype(BF16)
            uw = _dot(tinv.astype(BF16), rhs)
            u = uw[:, :LANES]
            w = uw[:, LANES:]

            s_old = s_ref[h]
            lhs = jnp.concatenate([w, q * egc], axis=0).astype(BF16)
            wq = _dot(lhs, s_old.astype(BF16))
            v_new = u - wq[:C]
            v16 = v_new.astype(BF16)
            o = wq[C:] + _dot(attn.astype(BF16), v16)
            k_dec = (k * jnp.exp(g_last - gcol)).astype(BF16)
            s_ref[h] = s_old * jnp.exp(g_last) + _dot_tn(k_dec, v16)

            zz = z_ref[r, h * LANES:(h + 1) * LANES]
            on = o * lax.rsqrt(jnp.mean(o * o, axis=-1, keepdims=True) + EPS)
            y = on * og_ref[...] * (zz * jax.nn.sigmoid(zz))
            y_ref[r, h * LANES:(h + 1) * LANES] = y.astype(y_ref.dtype)
        return carry

    lax.fori_loop(0, tb // C, chunk_body, 0)


def _deltanet(proj, ba, conv_w, adt, onorm_g, *, batch, seq, heads, tb):
    n = proj.shape[0]
    width = heads * LANES
    nt = seq // tb
    kernel = functools.partial(_deltanet_kernel, heads=heads, tb=tb)
    return pl.pallas_call(
        kernel,
        grid=(batch, nt),
        in_specs=[
            pl.BlockSpec((tb, 3 * width), lambda b, t: (b * nt + t, 0)),
            pl.BlockSpec((tb, width), lambda b, t: (b * nt + t, 3)),
            pl.BlockSpec((tb, LANES), lambda b, t: (b * nt + t, 0)),
            pl.BlockSpec(conv_w.shape, lambda b, t: (0, 0)),
            pl.BlockSpec(adt.shape, lambda b, t: (0, 0)),
            pl.BlockSpec((1, LANES), lambda b, t: (0, 0)),
        ],
        out_specs=pl.BlockSpec((tb, width), lambda b, t: (b * nt + t, 0)),
        out_shape=jax.ShapeDtypeStruct((n, width), BF16),
        scratch_shapes=[
            pltpu.VMEM((tb + HALO, 3 * width), F32),
            pltpu.VMEM((tb, 3 * width), F32),
            pltpu.VMEM((2 * heads, tb, LANES), F32),
            pltpu.VMEM((heads, LANES, LANES), F32),
        ],
        compiler_params=pltpu.CompilerParams(
            dimension_semantics=("arbitrary", "arbitrary"),
            vmem_limit_bytes=VMEM_LIMIT),
        name="deltanet",
    )(proj, proj, ba, conv_w, adt, onorm_g)


def _gelu(x):
    return 0.5 * x * (1.0 + lax.erf(x * (0.5 ** 0.5)))


def _gmlp_kernel(u_ref, v_ref, lg_ref, lb_ref, w_ref, bias_ref, y_ref, *, groups, tb):
    C = SG_CHUNK
    v = _gelu(v_ref[...])
    mu = jnp.mean(v, axis=-1, keepdims=True)
    vc = v - mu
    var = jnp.mean(vc * vc, axis=-1, keepdims=True)
    vg = (vc * lax.rsqrt(var + EPS) * lg_ref[...] + lb_ref[...]).astype(BF16)
    row = lax.broadcasted_iota(jnp.int32, (C, C), 0)
    col = lax.broadcasted_iota(jnp.int32, (C, C), 1)
    causal = row >= col
    for g in range(groups):
        cols = slice(g * LANES, (g + 1) * LANES)
        wg = jnp.where(causal, w_ref[g], 0.0).astype(BF16)
        for c in range(tb // C):
            rows = slice(c * C, (c + 1) * C)
            mixed = _dot(wg, vg[rows, cols]) + bias_ref[:, cols]
            y_ref[rows, cols] = (_gelu(u_ref[rows, cols]) * mixed).astype(y_ref.dtype)


def _gmlp(proj, ln_g, ln_b, sg_w, bias_full, *, groups, tb):
    n = proj.shape[0]
    width = groups * LANES
    kernel = functools.partial(_gmlp_kernel, groups=groups, tb=tb)
    return pl.pallas_call(
        kernel,
        grid=(n // tb,),
        in_specs=[
            pl.BlockSpec((tb, width), lambda i: (i, 4)),
            pl.BlockSpec((tb, width), lambda i: (i, 5)),
            pl.BlockSpec((1, width), lambda i: (0, 0)),
            pl.BlockSpec((1, width), lambda i: (0, 0)),
            pl.BlockSpec(sg_w.shape, lambda i: (0, 0, 0)),
            pl.BlockSpec(bias_full.shape, lambda i: (0, 0)),
        ],
        out_specs=pl.BlockSpec((tb, width), lambda i: (i, 0)),
        out_shape=jax.ShapeDtypeStruct((n, width), BF16),
        compiler_params=pltpu.CompilerParams(
            dimension_semantics=("arbitrary",),
            vmem_limit_bytes=VMEM_LIMIT),
        name="gmlp",
    )(proj, proj, ln_g, ln_b, sg_w, bias_full)


def _merge_kernel(ya_ref, yb_ref, ga_ref, gb_ref, x_ref, wa_ref, wb_ref, wo_ref, o_ref):
    ma = _dot(ya_ref[...], wa_ref[...])
    mb = _dot(yb_ref[...], wb_ref[...])
    merged = jax.nn.sigmoid(ga_ref[...]) * ma + jax.nn.sigmoid(gb_ref[...]) * mb
    o_ref[...] = x_ref[...] + _dot(merged.astype(BF16), wo_ref[...])


def _merge(ya, yb, proj, x, wa, wb, wo, *, tm):
    n, d = x.shape
    width = ya.shape[1]
    gate_block = (3 * width + width + 2 * width) // d
    const = dict(pipeline_mode=pl.Buffered(1))
    return pl.pallas_call(
        _merge_kernel,
        grid=(n // tm,),
        in_specs=[
            pl.BlockSpec((tm, width), lambda i: (i, 0)),
            pl.BlockSpec((tm, width), lambda i: (i, 0)),
            pl.BlockSpec((tm, d), lambda i: (i, gate_block)),
            pl.BlockSpec((tm, d), lambda i: (i, gate_block + 1)),
            pl.BlockSpec((tm, d), lambda i: (i, 0)),
            pl.BlockSpec(wa.shape, lambda i: (0, 0), **const),
            pl.BlockSpec(wb.shape, lambda i: (0, 0), **const),
            pl.BlockSpec(wo.shape, lambda i: (0, 0), **const),
        ],
        out_specs=pl.BlockSpec((tm, d), lambda i: (i, 0)),
        out_shape=jax.ShapeDtypeStruct((n, d), F32),
        compiler_params=pltpu.CompilerParams(
            dimension_semantics=("arbitrary",),
            vmem_limit_bytes=VMEM_LIMIT),
        name="merge_out",
    )(ya, yb, proj, proj, x, wa, wb, wo)


def _ffn_kernel(x_ref, g_ref, wg_ref, wu_ref, cw_ref, cb_ref, wd_ref, fg_ref, o_ref,
                h_ref, ge_ref, carry_ref, *, tm, final_norm):
    t = pl.program_id(1)
    k = pl.program_id(2)
    nk = pl.num_programs(2)
    kconv = cw_ref.shape[0]

    @pl.when(k == 0)
    def _():
        _rmsnorm_rows(x_ref, g_ref, h_ref, min(256, tm))
        o_ref[...] = x_ref[...]

    @pl.when(t == 0)
    def _():
        carry_ref[k] = jnp.zeros(carry_ref.shape[1:], F32)

    h = h_ref[...]
    gp = _dot(h, wg_ref[...])
    up = _dot(h, wu_ref[...])
    ge_ref[0:HALO, :] = carry_ref[k]
    ge_ref[HALO:HALO + tm, :] = gp
    carry_ref[k] = gp[tm - HALO:tm, :]
    acc = cb_ref[...]
    for j in range(kconv):
        off = HALO - (kconv - 1) + j
        acc = acc + ge_ref[off:off + tm, :] * cw_ref[j:j + 1, :]
    act = (acc * jax.nn.sigmoid(acc) * up).astype(BF16)
    o_ref[...] += _dot(act, wd_ref[...])

    if final_norm:
        @pl.when(k == nk - 1)
        def _():
            _rmsnorm_rows(o_ref, fg_ref, o_ref, min(256, tm))


def _ffn(x, g, wg, wu, conv_w, conv_b, wd, fg, *, batch, seq, tm, tf, final_norm):
    n, d = x.shape
    f = wg.shape[1]
    nt = seq // tm
    nk = f // tf
    kernel = functools.partial(_ffn_kernel, tm=tm, final_norm=final_norm)
    return pl.pallas_call(
        kernel,
        grid=(batch, nt, nk),
        in_specs=[
            pl.BlockSpec((tm, d), lambda b, t, k: (b * nt + t, 0)),
            pl.BlockSpec((1, d), lambda b, t, k: (0, 0)),
            pl.BlockSpec((d, tf), lambda b, t, k: (0, k)),
            pl.BlockSpec((d, tf), lambda b, t, k: (0, k)),
            pl.BlockSpec((conv_w.shape[0], tf), lambda b, t, k: (0, k)),
            pl.BlockSpec((1, tf), lambda b, t, k: (0, k)),
            pl.BlockSpec((tf, d), lambda b, t, k: (k, 0)),
            pl.BlockSpec((1, d), lambda b, t, k: (0, 0)),
        ],
        out_specs=pl.BlockSpec((tm, d), lambda b, t, k: (b * nt + t, 0)),
        out_shape=jax.ShapeDtypeStruct((n, d), F32),
        scratch_shapes=[
            pltpu.VMEM((tm, d), BF16),
            pltpu.VMEM((tm + HALO, tf), F32),
            pltpu.VMEM((nk, HALO, tf), F32),
        ],
        compiler_params=pltpu.CompilerParams(
            dimension_semantics=("arbitrary", "arbitrary", "arbitrary"),
            vmem_limit_bytes=VMEM_LIMIT),
        name="convffn",
    )(x, g, wg, wu, conv_w, conv_b, wd, fg)


def _pick(n, pref):
    t = min(n, pref)
    assert n % t == 0, (n, t)
    return t


def kernel(x, norm1_g, w_in, dn_conv_w, dn_a_log, dn_dt_bias, dn_onorm_g, sg_ln_g, sg_ln_b, sg_w, sg_b, w_branch_a, w_branch_b, w_out, norm2_g, ffn_w_gate, ffn_w_up, ffn_conv_w, ffn_conv_b, ffn_w_down, final_norm_g):
    batch, seq, d = x.shape
    depth = w_in.shape[0]
    heads = dn_a_log.shape[1]
    groups = sg_w.shape[1]
    width = heads * LANES
    d_ff = ffn_w_gate.shape[2]
    n = batch * seq
    assert dn_onorm_g.shape[1] == LANES and sg_w.shape[2] == SG_CHUNK == LANES
    assert groups * LANES == width and d == 2 * width and 2 * heads <= LANES
    assert dn_conv_w.shape[1] - 1 <= HALO and ffn_conv_w.shape[1] - 1 <= HALO
    ba0 = 4 * width
    assert w_in.shape[2] == 6 * width + 2 * d + 2 * heads

    tm_in = _pick(n, 1024)
    tn_in = _pick(6 * width + 2 * d, 512)
    tb_dn = _pick(seq, 256)
    tb_sg = _pick(n, 256)
    tm_mg = _pick(n, 256)
    tm_ff = _pick(seq, 512)
    tf_ff = _pick(d_ff, 512)

    xf = x.reshape(n, d)
    fg = final_norm_g.reshape(1, d)
    for l in range(depth):
        wl = w_in[l]
        w_main = jnp.concatenate([wl[:, :ba0], wl[:, ba0 + 2 * heads:]], axis=1).astype(BF16)
        w_ba = jnp.pad(wl[:, ba0:ba0 + 2 * heads], ((0, 0), (0, LANES - 2 * heads))).astype(BF16)
        proj, ba = _inproj(xf, norm1_g[l].reshape(1, d), w_main, w_ba, tm=tm_in, tn=tn_in)

        adt = jnp.zeros((2, LANES), F32)
        adt = adt.at[0, heads:2 * heads].set(dn_a_log[l]).at[1, heads:2 * heads].set(dn_dt_bias[l])
        ya = _deltanet(proj, ba, dn_conv_w[l], adt, dn_onorm_g[l].reshape(1, LANES),
                       batch=batch, seq=seq, heads=heads, tb=tb_dn)

        bias_full = jnp.repeat(jnp.transpose(sg_b[l]), LANES, axis=1)
        yb = _gmlp(proj, sg_ln_g[l].reshape(1, width), sg_ln_b[l].reshape(1, width),
                   sg_w[l], bias_full, groups=groups, tb=tb_sg)

        xf = _merge(ya, yb, proj, xf, w_branch_a[l].astype(BF16), w_branch_b[l].astype(BF16),
                    w_out[l].astype(BF16), tm=tm_mg)

        xf = _ffn(xf, norm2_g[l].reshape(1, d), ffn_w_gate[l].astype(BF16),
                  ffn_w_up[l].astype(BF16), ffn_conv_w[l], ffn_conv_b[l].reshape(1, d_ff),
                  ffn_w_down[l].astype(BF16), fg, batch=batch, seq=seq, tm=tm_ff, tf=tf_ff,
                  final_norm=(l == depth - 1))
    return xf.reshape(batch, seq, d)
```

```python
import functools

import jax
import jax.numpy as jnp
from jax import lax
from jax.experimental import pallas as pl
from jax.experimental.pallas import tpu as pltpu

EPS = 1e-6
LANES = 128
DN_CHUNK = 64
SG_CHUNK = 128
HALO = 8
VMEM_LIMIT = 56 * 1024 * 1024

F32 = jnp.float32
BF16 = jnp.bfloat16


def _dot(a, b):
    return jnp.dot(a, b, preferred_element_type=F32)


def _dot_nt(a, b):
    return lax.dot_general(a, b, (((1,), (1,)), ((), ())), preferred_element_type=F32)


def _dot_tn(a, b):
    return lax.dot_general(a, b, (((0,), (0,)), ((), ())), preferred_element_type=F32)


def _split(a):
    hi = a.astype(BF16)
    lo = (a - hi.astype(F32)).astype(BF16)
    return hi, lo


def _dot_split(a_hl, b_hl):
    ah, al = a_hl
    bh, bl = b_hl
    return _dot(ah, bh) + (_dot(ah, bl) + _dot(al, bh))


def _split3(a):
    hi = a.astype(BF16)
    r = a - hi.astype(F32)
    mid = r.astype(BF16)
    lo = (r - mid.astype(F32)).astype(BF16)
    return hi, mid, lo


def _rmsnorm_rows(x_ref, g_ref, o_ref, rows):
    n = x_ref.shape[0] // rows

    def body(i, _):
        r = pl.ds(pl.multiple_of(i * rows, rows), rows)
        x = x_ref[r, :]
        ms = jnp.mean(x * x, axis=-1, keepdims=True)
        o_ref[r, :] = (x * lax.rsqrt(ms + EPS) * g_ref[...]).astype(o_ref.dtype)
        return 0

    lax.fori_loop(0, n, body, 0)


def _inproj_kernel(x_ref, g_ref, w_ref, wba_ref, o_ref, ba_ref, h_ref):
    @pl.when(pl.program_id(1) == 0)
    def _():
        _rmsnorm_rows(x_ref, g_ref, h_ref, min(256, x_ref.shape[0]))
        ba_ref[...] = _dot(h_ref[...], wba_ref[...])

    o_ref[...] = _dot(h_ref[...], w_ref[...])


def _inproj(x, g, w, wba, *, tm, tn):
    n, d = x.shape
    nc = w.shape[1]
    return pl.pallas_call(
        _inproj_kernel,
        grid=(n // tm, nc // tn),
        in_specs=[
            pl.BlockSpec((tm, d), lambda i, j: (i, 0)),
            pl.BlockSpec((1, d), lambda i, j: (0, 0)),
            pl.BlockSpec((d, tn), lambda i, j: (0, j)),
            pl.BlockSpec((d, LANES), lambda i, j: (0, 0)),
        ],
        out_specs=[
            pl.BlockSpec((tm, tn), lambda i, j: (i, j)),
            pl.BlockSpec((tm, LANES), lambda i, j: (i, 0)),
        ],
        out_shape=[
            jax.ShapeDtypeStruct((n, nc), F32),
            jax.ShapeDtypeStruct((n, LANES), F32),
        ],
        scratch_shapes=[pltpu.VMEM((tm, d), BF16)],
        compiler_params=pltpu.CompilerParams(
            dimension_semantics=("arbitrary", "arbitrary"),
            vmem_limit_bytes=VMEM_LIMIT),
        name="inproj",
    )(x, g, w, wba)


def _deltanet_kernel(qkv_ref, z_ref, ba_ref, cw_ref, adt_ref, og_ref, y_ref,
                     xe_ref, qkv_s, gb_s, gt_s, u_s, lhs_s, attn_s, kdec_s, s_ref, *, heads, tb):
    C = DN_CHUNK
    kconv = cw_ref.shape[0]

    @pl.when(pl.program_id(1) == 0)
    def _():
        xe_ref[0:HALO, :] = jnp.zeros((HALO, xe_ref.shape[1]), F32)
        s_ref[...] = jnp.zeros(s_ref.shape, F32)

    xe_ref[HALO:HALO + tb, :] = qkv_ref[...]
    scale = float(LANES) ** -0.5
    for c in range(3 * heads):
        cols = slice(c * LANES, (c + 1) * LANES)
        acc = None
        for j in range(kconv):
            off = HALO - (kconv - 1) + j
            term = xe_ref[off:off + tb, cols] * cw_ref[j:j + 1, cols]
            acc = term if acc is None else acc + term
        y = acc * jax.nn.sigmoid(acc)
        if c < 2 * heads:
            y = y * lax.rsqrt(jnp.sum(y * y, axis=-1, keepdims=True) + EPS)
            if c < heads:
                y = y * scale
        qkv_s[:, cols] = y
    xe_ref[0:HALO, :] = xe_ref[tb:tb + HALO, :]

    ba = ba_ref[...]
    beta_all = jax.nn.sigmoid(ba)
    xa = ba + adt_ref[1:2, :]
    softplus = jnp.maximum(xa, 0.0) + jnp.log1p(jnp.exp(-jnp.abs(xa)))
    g_all = -jnp.exp(adt_ref[0:1, :]) * softplus
    trow = lax.broadcasted_iota(jnp.int32, (tb, tb), 0)
    tcol = lax.broadcasted_iota(jnp.int32, (tb, tb), 1)
    chunk_tril = jnp.where(((trow // C) == (tcol // C)) & (trow >= tcol), 1.0, 0.0).astype(BF16)
    g3 = _split3(g_all)
    gc_all = _dot(chunk_tril, g3[0]) + (_dot(chunk_tril, g3[1]) + _dot(chunk_tril, g3[2]))
    gt_s[...] = gc_all.T
    lane = lax.broadcasted_iota(jnp.int32, (tb, LANES), 1)
    for h in range(heads):
        bcol = jnp.sum(jnp.where(lane == h, beta_all, 0.0), axis=-1, keepdims=True)
        gcol = jnp.sum(jnp.where(lane == heads + h, gc_all, 0.0), axis=-1, keepdims=True)
        gb_s[h] = jnp.broadcast_to(bcol, (tb, LANES))
        gb_s[heads + h] = jnp.broadcast_to(gcol, (tb, LANES))

    row = lax.broadcasted_iota(jnp.int32, (C, C), 0)
    col = lax.broadcasted_iota(jnp.int32, (C, C), 1)
    incl = row >= col
    strict = row > col
    eye = jnp.where(row == col, 1.0, 0.0).astype(F32)
    sizes = []
    s = 1
    while s < C:
        sizes.append(s)
        s *= 2
    level_masks = [
        ((row // (2 * s)) == (col // (2 * s))) & ((row & s) != 0) & ((col & s) == 0)
        for s in sizes
    ]
    nchunks = tb // C
    problems = [(c, h) for c in range(nchunks) for h in range(heads)]

    lmats = []
    for c, h in problems:
        rows = slice(c * C, (c + 1) * C)
        q = qkv_s[rows, h * LANES:(h + 1) * LANES]
        k = qkv_s[rows, (heads + h) * LANES:(heads + h + 1) * LANES]
        beta = gb_s[h, rows, :]
        gcol = gb_s[heads + h, rows, :]
        grow = gt_s[heads + h:heads + h + 1, rows]
        decay = jnp.where(incl, jnp.exp(gcol[:, :C] - grow), 0.0)
        kq = _dot_nt(jnp.concatenate([k * beta, q], axis=0).astype(BF16), k.astype(BF16))
        lmats.append(jnp.where(strict, kq[:C] * decay, 0.0))
        attn_s[h, rows, :] = (kq[C:] * decay).astype(BF16)

    tinvs = [eye - jnp.where(level_masks[0], lm, 0.0) for lm in lmats]
    for m in level_masks[1:]:
        t16 = [t.astype(BF16) for t in tinvs]
        tc = [_dot(t, jnp.where(m, lm, 0.0).astype(BF16)) for t, lm in zip(t16, lmats)]
        tinvs = [t - _dot(x.astype(BF16), t_b) for t, x, t_b in zip(tinvs, tc, t16)]

    for (c, h), tinv in zip(problems, tinvs):
        rows = slice(c * C, (c + 1) * C)
        q = qkv_s[rows, h * LANES:(h + 1) * LANES]
        k = qkv_s[rows, (heads + h) * LANES:(heads + h + 1) * LANES]
        v = qkv_s[rows, (2 * heads + h) * LANES:(2 * heads + h + 1) * LANES]
        beta = gb_s[h, rows, :]
        gcol = gb_s[heads + h, rows, :]
        egc = jnp.exp(gcol)
        rhs = jnp.concatenate([v * beta, k * beta * egc], axis=1).astype(BF16)
        uw = _dot(tinv.astype(BF16), rhs)
        u_s[h, rows, :] = uw[:, :LANES]
        lhs_s[h, 2 * c * C:(2 * c + 1) * C, :] = uw[:, LANES:].astype(BF16)
        lhs_s[h, (2 * c + 1) * C:(2 * c + 2) * C, :] = (q * egc).astype(BF16)
        g_last = gcol[C - 1:C, :]
        kdec_s[h, rows, :] = (k * jnp.exp(g_last - gcol)).astype(BF16)

    for c in range(nchunks):
        rows = slice(c * C, (c + 1) * C)
        s_old = [s_ref[h] for h in range(heads)]
        wq = [_dot(lhs_s[h, 2 * c * C:(2 * c + 2) * C, :], s_old[h].astype(BF16))
              for h in range(heads)]
        v16 = [(u_s[h, rows, :] - wq[h][:C]).astype(BF16) for h in range(heads)]
        for h in range(heads):
            g_last = gb_s[heads + h, (c + 1) * C - 1:(c + 1) * C, :]
            s_ref[h] = s_old[h] * jnp.exp(g_last) + _dot_tn(kdec_s[h, rows, :], v16[h])
        for h in range(heads):
            o = wq[h][C:] + _dot(attn_s[h, rows, :], v16[h])
            zz = z_ref[rows, h * LANES:(h + 1) * LANES]
            on = o * lax.rsqrt(jnp.mean(o * o, axis=-1, keepdims=True) + EPS)
            y = on * og_ref[...] * (zz * jax.nn.sigmoid(zz))
            y_ref[rows, h * LANES:(h + 1) * LANES] = y.astype(y_ref.dtype)


def _deltanet(proj, ba, conv_w, adt, onorm_g, *, batch, seq, heads, tb):
    n = proj.shape[0]
    width = heads * LANES
    nt = seq // tb
    kernel = functools.partial(_deltanet_kernel, heads=heads, tb=tb)
    return pl.pallas_call(
        kernel,
        grid=(batch, nt),
        in_specs=[
            pl.BlockSpec((tb, 3 * width), lambda b, t: (b * nt + t, 0)),
            pl.BlockSpec((tb, width), lambda b, t: (b * nt + t, 3)),
            pl.BlockSpec((tb, LANES), lambda b, t: (b * nt + t, 0)),
            pl.BlockSpec(conv_w.shape, lambda b, t: (0, 0)),
            pl.BlockSpec(adt.shape, lambda b, t: (0, 0)),
            pl.BlockSpec((1, LANES), lambda b, t: (0, 0)),
        ],
        out_specs=pl.BlockSpec((tb, width), lambda b, t: (b * nt + t, 0)),
        out_shape=jax.ShapeDtypeStruct((n, width), BF16),
        scratch_shapes=[
            pltpu.VMEM((tb + HALO, 3 * width), F32),
            pltpu.VMEM((tb, 3 * width), F32),
            pltpu.VMEM((2 * heads, tb, LANES), F32),
            pltpu.VMEM((LANES, tb), F32),
            pltpu.VMEM((heads, tb, LANES), F32),
            pltpu.VMEM((heads, 2 * tb, LANES), BF16),
            pltpu.VMEM((heads, tb, DN_CHUNK), BF16),
            pltpu.VMEM((heads, tb, LANES), BF16),
            pltpu.VMEM((heads, LANES, LANES), F32),
        ],
        compiler_params=pltpu.CompilerParams(
            dimension_semantics=("arbitrary", "arbitrary"),
            vmem_limit_bytes=VMEM_LIMIT),
        name="deltanet",
    )(proj, proj, ba, conv_w, adt, onorm_g)


def _gelu(x):
    return 0.5 * x * (1.0 + lax.erf(x * (0.5 ** 0.5)))


def _gmlp_kernel(u_ref, v_ref, lg_ref, lb_ref, w_ref, bias_ref, y_ref, *, groups, tb):
    C = SG_CHUNK
    v = _gelu(v_ref[...])
    mu = jnp.mean(v, axis=-1, keepdims=True)
    vc = v - mu
    var = jnp.mean(vc * vc, axis=-1, keepdims=True)
    vg = (vc * lax.rsqrt(var + EPS) * lg_ref[...] + lb_ref[...]).astype(BF16)
    row = lax.broadcasted_iota(jnp.int32, (C, C), 0)
    col = lax.broadcasted_iota(jnp.int32, (C, C), 1)
    causal = row >= col
    for g in range(groups):
        cols = slice(g * LANES, (g + 1) * LANES)
        wg = jnp.where(causal, w_ref[g], 0.0).astype(BF16)
        for c in range(tb // C):
            rows = slice(c * C, (c + 1) * C)
            mixed = _dot(wg, vg[rows, cols]) + bias_ref[:, cols]
            y_ref[rows, cols] = (_gelu(u_ref[rows, cols]) * mixed).astype(y_ref.dtype)


def _gmlp(proj, ln_g, ln_b, sg_w, bias_full, *, groups, tb):
    n = proj.shape[0]
    width = groups * LANES
    kernel = functools.partial(_gmlp_kernel, groups=groups, tb=tb)
    return pl.pallas_call(
        kernel,
        grid=(n // tb,),
        in_specs=[
            pl.BlockSpec((tb, width), lambda i: (i, 4)),
            pl.BlockSpec((tb, width), lambda i: (i, 5)),
            pl.BlockSpec((1, width), lambda i: (0, 0)),
            pl.BlockSpec((1, width), lambda i: (0, 0)),
            pl.BlockSpec(sg_w.shape, lambda i: (0, 0, 0)),
            pl.BlockSpec(bias_full.shape, lambda i: (0, 0)),
        ],
        out_specs=pl.BlockSpec((tb, width), lambda i: (i, 0)),
        out_shape=jax.ShapeDtypeStruct((n, width), BF16),
        compiler_params=pltpu.CompilerParams(
            dimension_semantics=("arbitrary",),
            vmem_limit_bytes=VMEM_LIMIT),
        name="gmlp",
    )(proj, proj, ln_g, ln_b, sg_w, bias_full)


def _merge_kernel(ya_ref, yb_ref, ga_ref, gb_ref, x_ref, wa_ref, wb_ref, wo_ref, o_ref):
    ma = _dot(ya_ref[...], wa_ref[...])
    mb = _dot(yb_ref[...], wb_ref[...])
    merged = jax.nn.sigmoid(ga_ref[...]) * ma + jax.nn.sigmoid(gb_ref[...]) * mb
    o_ref[...] = x_ref[...] + _dot(merged.astype(BF16), wo_ref[...])


def _merge(ya, yb, proj, x, wa, wb, wo, *, tm):
    n, d = x.shape
    width = ya.shape[1]
    gate_block = (3 * width + width + 2 * width) // d
    const = dict(pipeline_mode=pl.Buffered(1))
    return pl.pallas_call(
        _merge_kernel,
        grid=(n // tm,),
        in_specs=[
            pl.BlockSpec((tm, width), lambda i: (i, 0)),
            pl.BlockSpec((tm, width), lambda i: (i, 0)),
            pl.BlockSpec((tm, d), lambda i: (i, gate_block)),
            pl.BlockSpec((tm, d), lambda i: (i, gate_block + 1)),
            pl.BlockSpec((tm, d), lambda i: (i, 0)),
            pl.BlockSpec(wa.shape, lambda i: (0, 0), **const),
            pl.BlockSpec(wb.shape, lambda i: (0, 0), **const),
            pl.BlockSpec(wo.shape, lambda i: (0, 0), **const),
        ],
        out_specs=pl.BlockSpec((tm, d), lambda i: (i, 0)),
        out_shape=jax.ShapeDtypeStruct((n, d), F32),
        compiler_params=pltpu.CompilerParams(
            dimension_semantics=("arbitrary",),
            vmem_limit_bytes=VMEM_LIMIT),
        name="merge_out",
    )(ya, yb, proj, proj, x, wa, wb, wo)


def _ffn_kernel(x_ref, g_ref, wg_ref, wu_ref, cw_ref, cb_ref, wd_ref, fg_ref, o_ref,
                h_ref, ge_ref, carry_ref, *, tm, final_norm):
    t = pl.program_id(1)
    k = pl.program_id(2)
    nk = pl.num_programs(2)
    kconv = cw_ref.shape[0]

    @pl.when(k == 0)
    def _():
        _rmsnorm_rows(x_ref, g_ref, h_ref, min(256, tm))
        o_ref[...] = x_ref[...]

    @pl.when(t == 0)
    def _():
        carry_ref[k] = jnp.zeros(carry_ref.shape[1:], F32)

    h = h_ref[...]
    gp = _dot(h, wg_ref[...])
    up = _dot(h, wu_ref[...])
    ge_ref[0:HALO, :] = carry_ref[k]
    ge_ref[HALO:HALO + tm, :] = gp
    carry_ref[k] = gp[tm - HALO:tm, :]
    acc = cb_ref[...]
    for j in range(kconv):
        off = HALO - (kconv - 1) + j
        acc = acc + ge_ref[off:off + tm, :] * cw_ref[j:j + 1, :]
    act = (acc * jax.nn.sigmoid(acc) * up).astype(BF16)
    o_ref[...] += _dot(act, wd_ref[...])

    if final_norm:
        @pl.when(k == nk - 1)
        def _():
            _rmsnorm_rows(o_ref, fg_ref, o_ref, min(256, tm))


def _ffn(x, g, wg, wu, conv_w, conv_b, wd, fg, *, batch, seq, tm, tf, final_norm):
    n, d = x.shape
    f = wg.shape[1]
    nt = seq // tm
    nk = f // tf
    kernel = functools.partial(_ffn_kernel, tm=tm, final_norm=final_norm)
    return pl.pallas_call(
        kernel,
        grid=(batch, nt, nk),
        in_specs=[
            pl.BlockSpec((tm, d), lambda b, t, k: (b * nt + t, 0)),
            pl.BlockSpec((1, d), lambda b, t, k: (0, 0)),
            pl.BlockSpec((d, tf), lambda b, t, k: (0, k)),
            pl.BlockSpec((d, tf), lambda b, t, k: (0, k)),
            pl.BlockSpec((conv_w.shape[0], tf), lambda b, t, k: (0, k)),
            pl.BlockSpec((1, tf), lambda b, t, k: (0, k)),
            pl.BlockSpec((tf, d), lambda b, t, k: (k, 0)),
            pl.BlockSpec((1, d), lambda b, t, k: (0, 0)),
        ],
        out_specs=pl.BlockSpec((tm, d), lambda b, t, k: (b * nt + t, 0)),
        out_shape=jax.ShapeDtypeStruct((n, d), F32),
        scratch_shapes=[
            pltpu.VMEM((tm, d), BF16),
            pltpu.VMEM((tm + HALO, tf), F32),
            pltpu.VMEM((nk, HALO, tf), F32),
        ],
        compiler_params=pltpu.CompilerParams(
            dimension_semantics=("arbitrary", "arbitrary", "arbitrary"),
            vmem_limit_bytes=VMEM_LIMIT),
        name="convffn",
    )(x, g, wg, wu, conv_w, conv_b, wd, fg)


def _pick(n, pref):
    t = min(n, pref)
    assert n % t == 0, (n, t)
    return t


def kernel(x, norm1_g, w_in, dn_conv_w, dn_a_log, dn_dt_bias, dn_onorm_g, sg_ln_g, sg_ln_b, sg_w, sg_b, w_branch_a, w_branch_b, w_out, norm2_g, ffn_w_gate, ffn_w_up, ffn_conv_w, ffn_conv_b, ffn_w_down, final_norm_g):
    batch, seq, d = x.shape
    depth = w_in.shape[0]
    heads = dn_a_log.shape[1]
    groups = sg_w.shape[1]
    width = heads * LANES
    d_ff = ffn_w_gate.shape[2]
    n = batch * seq
    assert dn_onorm_g.shape[1] == LANES and sg_w.shape[2] == SG_CHUNK == LANES
    assert groups * LANES == width and d == 2 * width and 2 * heads <= LANES
    assert dn_conv_w.shape[1] - 1 <= HALO and ffn_conv_w.shape[1] - 1 <= HALO
    ba0 = 4 * width
    assert w_in.shape[2] == 6 * width + 2 * d + 2 * heads

    tm_in = _pick(n, 1024)
    tn_in = _pick(6 * width + 2 * d, 512)
    tb_dn = _pick(seq, 256)
    tb_sg = _pick(n, 256)
    tm_mg = _pick(n, 256)
    tm_ff = _pick(seq, 512)
    tf_ff = _pick(d_ff, 512)

    xf = x.reshape(n, d)
    fg = final_norm_g.reshape(1, d)
    for l in range(depth):
        wl = w_in[l]
        w_main = jnp.concatenate([wl[:, :ba0], wl[:, ba0 + 2 * heads:]], axis=1).astype(BF16)
        w_ba = jnp.pad(wl[:, ba0:ba0 + 2 * heads], ((0, 0), (0, LANES - 2 * heads))).astype(BF16)
        proj, ba = _inproj(xf, norm1_g[l].reshape(1, d), w_main, w_ba, tm=tm_in, tn=tn_in)

        adt = jnp.zeros((2, LANES), F32)
        adt = adt.at[0, heads:2 * heads].set(dn_a_log[l]).at[1, heads:2 * heads].set(dn_dt_bias[l])
        ya = _deltanet(proj, ba, dn_conv_w[l], adt, dn_onorm_g[l].reshape(1, LANES),
                       batch=batch, seq=seq, heads=heads, tb=tb_dn)

        bias_full = jnp.repeat(jnp.transpose(sg_b[l]), LANES, axis=1)
        yb = _gmlp(proj, sg_ln_g[l].reshape(1, width), sg_ln_b[l].reshape(1, width),
                   sg_w[l], bias_full, groups=groups, tb=tb_sg)

        xf = _merge(ya, yb, proj, xf, w_branch_a[l].astype(BF16), w_branch_b[l].astype(BF16),
                    w_out[l].astype(BF16), tm=tm_mg)

        xf = _ffn(xf, norm2_g[l].reshape(1, d), ffn_w_gate[l].astype(BF16),
                  ffn_w_up[l].astype(BF16), ffn_conv_w[l], ffn_conv_b[l].reshape(1, d_ff),
                  ffn_w_down[l].astype(BF16), fg, batch=batch, seq=seq, tm=tm_ff, tf=tf_ff,
                  final_norm=(l == depth - 1))
    return xf.reshape(batch, seq, d)
```

```python
import functools

import jax
import jax.numpy as jnp
from jax import lax
from jax.experimental import pallas as pl
from jax.experimental.pallas import tpu as pltpu

EPS = 1e-6
LANES = 128
DN_CHUNK = 64
SG_CHUNK = 128
HALO = 8
VMEM_LIMIT = 56 * 1024 * 1024

F32 = jnp.float32
BF16 = jnp.bfloat16


def _dot(a, b):
    return jnp.dot(a, b, preferred_element_type=F32)


def _dot_nt(a, b):
    return lax.dot_general(a, b, (((1,), (1,)), ((), ())), preferred_element_type=F32)


def _dot_tn(a, b):
    return lax.dot_general(a, b, (((0,), (0,)), ((), ())), preferred_element_type=F32)


def _split(a):
    hi = a.astype(BF16)
    lo = (a - hi.astype(F32)).astype(BF16)
    return hi, lo


def _dot_split(a_hl, b_hl):
    ah, al = a_hl
    bh, bl = b_hl
    return _dot(ah, bh) + (_dot(ah, bl) + _dot(al, bh))


def _split3(a):
    hi = a.astype(BF16)
    r = a - hi.astype(F32)
    mid = r.astype(BF16)
    lo = (r - mid.astype(F32)).astype(BF16)
    return hi, mid, lo


def _rmsnorm_rows(x_ref, g_ref, o_ref, rows):
    n = x_ref.shape[0] // rows

    def body(i, _):
        r = pl.ds(pl.multiple_of(i * rows, rows), rows)
        x = x_ref[r, :]
        ms = jnp.mean(x * x, axis=-1, keepdims=True)
        o_ref[r, :] = (x * lax.rsqrt(ms + EPS) * g_ref[...]).astype(o_ref.dtype)
        return 0

    lax.fori_loop(0, n, body, 0)


def _inproj_kernel(x_ref, g_ref, w_ref, wba_ref, o_ref, ba_ref, h_ref):
    @pl.when(pl.program_id(1) == 0)
    def _():
        _rmsnorm_rows(x_ref, g_ref, h_ref, min(256, x_ref.shape[0]))
        ba_ref[...] = _dot(h_ref[...], wba_ref[...])

    o_ref[...] = _dot(h_ref[...], w_ref[...]).astype(o_ref.dtype)


def _inproj(x, g, w, wba, layer, *, tm, tn):
    n, d = x.shape
    nc = w.shape[2]
    return pl.pallas_call(
        _inproj_kernel,
        grid=(n // tm, nc // tn),
        in_specs=[
            pl.BlockSpec((tm, d), lambda i, j: (i, 0)),
            pl.BlockSpec((None, 1, d), lambda i, j: (layer, 0, 0)),
            pl.BlockSpec((None, d, tn), lambda i, j: (layer, 0, j)),
            pl.BlockSpec((None, d, LANES), lambda i, j: (layer, 0, 0)),
        ],
        out_specs=[
            pl.BlockSpec((tm, tn), lambda i, j: (i, j)),
            pl.BlockSpec((tm, LANES), lambda i, j: (i, 0)),
        ],
        out_shape=[
            jax.ShapeDtypeStruct((n, nc), BF16),
            jax.ShapeDtypeStruct((n, LANES), F32),
        ],
        scratch_shapes=[pltpu.VMEM((tm, d), BF16)],
        compiler_params=pltpu.CompilerParams(
            dimension_semantics=("arbitrary", "arbitrary"),
            vmem_limit_bytes=VMEM_LIMIT),
        name="inproj",
    )(x, g, w, wba)


def _deltanet_kernel(qkv_ref, z_ref, ba_ref, cw_ref, adt_ref, og_ref, y_ref,
                     xe_ref, qkv_s, gb_s, gt_s, u_s, lhs_s, attn_s, kdec_s, s_ref, *, heads, tb):
    C = DN_CHUNK
    kconv = cw_ref.shape[0]

    @pl.when(pl.program_id(1) == 0)
    def _():
        xe_ref[0:HALO, :] = jnp.zeros((HALO, xe_ref.shape[1]), F32)
        s_ref[...] = jnp.zeros(s_ref.shape, F32)

    xe_ref[HALO:HALO + tb, :] = qkv_ref[...].astype(F32)
    scale = float(LANES) ** -0.5
    for c in range(3 * heads):
        cols = slice(c * LANES, (c + 1) * LANES)
        acc = None
        for j in range(kconv):
            off = HALO - (kconv - 1) + j
            term = xe_ref[off:off + tb, cols] * cw_ref[j:j + 1, cols]
            acc = term if acc is None else acc + term
        y = acc * jax.nn.sigmoid(acc)
        if c < 2 * heads:
            y = y * lax.rsqrt(jnp.sum(y * y, axis=-1, keepdims=True) + EPS)
            if c < heads:
                y = y * scale
        qkv_s[:, cols] = y
    xe_ref[0:HALO, :] = xe_ref[tb:tb + HALO, :]

    ba = ba_ref[...]
    beta_all = jax.nn.sigmoid(ba)
    xa = ba + adt_ref[1:2, :]
    softplus = jnp.maximum(xa, 0.0) + jnp.log1p(jnp.exp(-jnp.abs(xa)))
    g_all = -jnp.exp(adt_ref[0:1, :]) * softplus
    trow = lax.broadcasted_iota(jnp.int32, (tb, tb), 0)
    tcol = lax.broadcasted_iota(jnp.int32, (tb, tb), 1)
    chunk_tril = jnp.where(((trow // C) == (tcol // C)) & (trow >= tcol), 1.0, 0.0).astype(BF16)
    g3 = _split3(g_all)
    gc_all = _dot(chunk_tril, g3[0]) + (_dot(chunk_tril, g3[1]) + _dot(chunk_tril, g3[2]))
    gt_s[...] = gc_all.T
    lane = lax.broadcasted_iota(jnp.int32, (tb, LANES), 1)
    for h in range(heads):
        bcol = jnp.sum(jnp.where(lane == h, beta_all, 0.0), axis=-1, keepdims=True)
        gcol = jnp.sum(jnp.where(lane == heads + h, gc_all, 0.0), axis=-1, keepdims=True)
        gb_s[h] = jnp.broadcast_to(bcol, (tb, LANES))
        gb_s[heads + h] = jnp.broadcast_to(gcol, (tb, LANES))

    row = lax.broadcasted_iota(jnp.int32, (C, C), 0)
    col = lax.broadcasted_iota(jnp.int32, (C, C), 1)
    incl = row >= col
    strict = row > col
    eye = jnp.where(row == col, 1.0, 0.0).astype(F32)
    sizes = []
    s = 1
    while s < C:
        sizes.append(s)
        s *= 2
    level_masks = [
        ((row // (2 * s)) == (col // (2 * s))) & ((row & s) != 0) & ((col & s) == 0)
        for s in sizes
    ]
    nchunks = tb // C
    problems = [(c, h) for c in range(nchunks) for h in range(heads)]

    lmats = []
    for c, h in problems:
        rows = slice(c * C, (c + 1) * C)
        q = qkv_s[rows, h * LANES:(h + 1) * LANES]
        k = qkv_s[rows, (heads + h) * LANES:(heads + h + 1) * LANES]
        beta = gb_s[h, rows, :]
        gcol = gb_s[heads + h, rows, :]
        grow = gt_s[heads + h:heads + h + 1, rows]
        decay = jnp.where(incl, jnp.exp(gcol[:, :C] - grow), 0.0)
        kq = _dot_nt(jnp.concatenate([k * beta, q], axis=0).astype(BF16), k.astype(BF16))
        lmats.append(jnp.where(strict, kq[:C] * decay, 0.0))
        attn_s[h, rows, :] = (kq[C:] * decay).astype(BF16)

    tinvs = [eye - jnp.where(level_masks[0], lm, 0.0) for lm in lmats]
    for m in level_masks[1:]:
        t16 = [t.astype(BF16) for t in tinvs]
        tc = [_dot(t, jnp.where(m, lm, 0.0).astype(BF16)) for t, lm in zip(t16, lmats)]
        tinvs = [t - _dot(x.astype(BF16), t_b) for t, x, t_b in zip(tinvs, tc, t16)]

    for (c, h), tinv in zip(problems, tinvs):
        rows = slice(c * C, (c + 1) * C)
        q = qkv_s[rows, h * LANES:(h + 1) * LANES]
        k = qkv_s[rows, (heads + h) * LANES:(heads + h + 1) * LANES]
        v = qkv_s[rows, (2 * heads + h) * LANES:(2 * heads + h + 1) * LANES]
        beta = gb_s[h, rows, :]
        gcol = gb_s[heads + h, rows, :]
        egc = jnp.exp(gcol)
        rhs = jnp.concatenate([v * beta, k * beta * egc], axis=1).astype(BF16)
        uw = _dot(tinv.astype(BF16), rhs)
        u_s[h, rows, :] = uw[:, :LANES]
        lhs_s[h, 2 * c * C:(2 * c + 1) * C, :] = uw[:, LANES:].astype(BF16)
        lhs_s[h, (2 * c + 1) * C:(2 * c + 2) * C, :] = (q * egc).astype(BF16)
        g_last = gcol[C - 1:C, :]
        kdec_s[h, rows, :] = (k * jnp.exp(g_last - gcol)).astype(BF16)

    for c in range(nchunks):
        rows = slice(c * C, (c + 1) * C)
        s_old = [s_ref[h] for h in range(heads)]
        wq = [_dot(lhs_s[h, 2 * c * C:(2 * c + 2) * C, :], s_old[h].astype(BF16))
              for h in range(heads)]
        v16 = [(u_s[h, rows, :] - wq[h][:C]).astype(BF16) for h in range(heads)]
        for h in range(heads):
            g_last = gb_s[heads + h, (c + 1) * C - 1:(c + 1) * C, :]
            s_ref[h] = s_old[h] * jnp.exp(g_last) + _dot_tn(kdec_s[h, rows, :], v16[h])
        for h in range(heads):
            o = wq[h][C:] + _dot(attn_s[h, rows, :], v16[h])
            zz = z_ref[rows, h * LANES:(h + 1) * LANES].astype(F32)
            on = o * lax.rsqrt(jnp.mean(o * o, axis=-1, keepdims=True) + EPS)
            y = on * og_ref[...] * (zz * jax.nn.sigmoid(zz))
            y_ref[rows, h * LANES:(h + 1) * LANES] = y.astype(y_ref.dtype)


def _deltanet(proj, ba, conv_w, adt, onorm_g, *, batch, seq, heads, tb):
    n = proj.shape[0]
    width = heads * LANES
    nt = seq // tb
    kernel = functools.partial(_deltanet_kernel, heads=heads, tb=tb)
    return pl.pallas_call(
        kernel,
        grid=(batch, nt),
        in_specs=[
            pl.BlockSpec((tb, 3 * width), lambda b, t: (b * nt + t, 0)),
            pl.BlockSpec((tb, width), lambda b, t: (b * nt + t, 3)),
            pl.BlockSpec((tb, LANES), lambda b, t: (b * nt + t, 0)),
            pl.BlockSpec(conv_w.shape, lambda b, t: (0, 0)),
            pl.BlockSpec(adt.shape, lambda b, t: (0, 0)),
            pl.BlockSpec((1, LANES), lambda b, t: (0, 0)),
        ],
        out_specs=pl.BlockSpec((tb, width), lambda b, t: (b * nt + t, 0)),
        out_shape=jax.ShapeDtypeStruct((n, width), BF16),
        scratch_shapes=[
            pltpu.VMEM((tb + HALO, 3 * width), F32),
            pltpu.VMEM((tb, 3 * width), F32),
            pltpu.VMEM((2 * heads, tb, LANES), F32),
            pltpu.VMEM((LANES, tb), F32),
            pltpu.VMEM((heads, tb, LANES), F32),
            pltpu.VMEM((heads, 2 * tb, LANES), BF16),
            pltpu.VMEM((heads, tb, DN_CHUNK), BF16),
            pltpu.VMEM((heads, tb, LANES), BF16),
            pltpu.VMEM((heads, LANES, LANES), F32),
        ],
        compiler_params=pltpu.CompilerParams(
            dimension_semantics=("arbitrary", "arbitrary"),
            vmem_limit_bytes=VMEM_LIMIT),
        name="deltanet",
    )(proj, proj, ba, conv_w, adt, onorm_g)


def _gelu(x):
    return 0.5 * x * (1.0 + lax.erf(x * (0.5 ** 0.5)))


def _merge_kernel(ya_ref, u_ref, v_ref, ga_ref, gb_ref, x_ref, lg_ref, lb_ref, sw_ref, bias_ref,
                  wa_ref, wb_ref, wo_ref, o_ref, yb_s, *, groups, tm):
    C = SG_CHUNK
    merged_a = jax.nn.sigmoid(ga_ref[...].astype(F32)) * _dot(ya_ref[...], wa_ref[...])

    v = _gelu(v_ref[...].astype(F32))
    mu = jnp.mean(v, axis=-1, keepdims=True)
    vc = v - mu
    var = jnp.mean(vc * vc, axis=-1, keepdims=True)
    vg = (vc * lax.rsqrt(var + EPS) * lg_ref[...] + lb_ref[...]).astype(BF16)
    row = lax.broadcasted_iota(jnp.int32, (C, C), 0)
    col = lax.broadcasted_iota(jnp.int32, (C, C), 1)
    causal = row >= col
    for g in range(groups):
        cols = slice(g * LANES, (g + 1) * LANES)
        wg = jnp.where(causal, sw_ref[g], 0.0).astype(BF16)
        for c in range(tm // C):
            rows = slice(c * C, (c + 1) * C)
            mixed = _dot(wg, vg[rows, cols]) + bias_ref[:, cols]
            yb_s[rows, cols] = (_gelu(u_ref[rows, cols].astype(F32)) * mixed).astype(BF16)

    mb = _dot(yb_s[...], wb_ref[...])
    merged = merged_a + jax.nn.sigmoid(gb_ref[...].astype(F32)) * mb
    o_ref[...] = x_ref[...] + _dot(merged.astype(BF16), wo_ref[...])


def _merge(ya, proj, x, ln_g, ln_b, sg_w, bias_full, wa, wb, wo, layer, *, groups, tm):
    n, d = x.shape
    width = ya.shape[1]
    gate_block = (6 * width) // d
    const = dict(pipeline_mode=pl.Buffered(1))
    kernel = functools.partial(_merge_kernel, groups=groups, tm=tm)
    return pl.pallas_call(
        kernel,
        grid=(n // tm,),
        in_specs=[
            pl.BlockSpec((tm, width), lambda i: (i, 0)),
            pl.BlockSpec((tm, width), lambda i: (i, 4)),
            pl.BlockSpec((tm, width), lambda i: (i, 5)),
            pl.BlockSpec((tm, d), lambda i: (i, gate_block)),
            pl.BlockSpec((tm, d), lambda i: (i, gate_block + 1)),
            pl.BlockSpec((tm, d), lambda i: (i, 0)),
            pl.BlockSpec((None, 1, width), lambda i: (layer, 0, 0)),
            pl.BlockSpec((None, 1, width), lambda i: (layer, 0, 0)),
            pl.BlockSpec((None,) + sg_w.shape[1:], lambda i: (layer, 0, 0, 0)),
            pl.BlockSpec(bias_full.shape, lambda i: (0, 0)),
            pl.BlockSpec((None,) + wa.shape[1:], lambda i: (layer, 0, 0), **const),
            pl.BlockSpec((None,) + wb.shape[1:], lambda i: (layer, 0, 0), **const),
            pl.BlockSpec((None,) + wo.shape[1:], lambda i: (layer, 0, 0), **const),
        ],
        out_specs=pl.BlockSpec((tm, d), lambda i: (i, 0)),
        out_shape=jax.ShapeDtypeStruct((n, d), F32),
        scratch_shapes=[pltpu.VMEM((tm, width), BF16)],
        compiler_params=pltpu.CompilerParams(
            dimension_semantics=("arbitrary",),
            vmem_limit_bytes=VMEM_LIMIT),
        name="gmlp_merge_out",
    )(ya, proj, proj, proj, proj, x, ln_g, ln_b, sg_w, bias_full, wa, wb, wo)


def _ffn_kernel(x_ref, g_ref, wg_ref, wu_ref, cw_ref, cb_ref, wd_ref, fg_ref, o_ref,
                h_ref, ge_ref, carry_ref, *, tm, final_norm):
    t = pl.program_id(1)
    k = pl.program_id(2)
    nk = pl.num_programs(2)
    kconv = cw_ref.shape[0]

    @pl.when(k == 0)
    def _():
        _rmsnorm_rows(x_ref, g_ref, h_ref, min(256, tm))
        o_ref[...] = x_ref[...]

    @pl.when(t == 0)
    def _():
        carry_ref[k] = jnp.zeros(carry_ref.shape[1:], F32)

    h = h_ref[...]
    gp = _dot(h, wg_ref[...])
    up = _dot(h, wu_ref[...])
    ge_ref[0:HALO, :] = carry_ref[k]
    ge_ref[HALO:HALO + tm, :] = gp
    carry_ref[k] = gp[tm - HALO:tm, :]
    acc = cb_ref[...]
    for j in range(kconv):
        off = HALO - (kconv - 1) + j
        acc = acc + ge_ref[off:off + tm, :] * cw_ref[j:j + 1, :]
    act = (acc * jax.nn.sigmoid(acc) * up).astype(BF16)
    o_ref[...] += _dot(act, wd_ref[...])

    if final_norm:
        @pl.when(k == nk - 1)
        def _():
            _rmsnorm_rows(o_ref, fg_ref, o_ref, min(256, tm))


def _ffn(x, g, wg, wu, conv_w, conv_b, wd, fg, layer, *, batch, seq, tm, tf, final_norm):
    n, d = x.shape
    f = wg.shape[2]
    nt = seq // tm
    nk = f // tf
    kernel = functools.partial(_ffn_kernel, tm=tm, final_norm=final_norm)
    return pl.pallas_call(
        kernel,
        grid=(batch, nt, nk),
        in_specs=[
            pl.BlockSpec((tm, d), lambda b, t, k: (b * nt + t, 0)),
            pl.BlockSpec((None, 1, d), lambda b, t, k: (layer, 0, 0)),
            pl.BlockSpec((None, d, tf), lambda b, t, k: (layer, 0, k)),
            pl.BlockSpec((None, d, tf), lambda b, t, k: (layer, 0, k)),
            pl.BlockSpec((None, conv_w.shape[1], tf), lambda b, t, k: (layer, 0, k)),
            pl.BlockSpec((None, 1, tf), lambda b, t, k: (layer, 0, k)),
            pl.BlockSpec((None, tf, d), lambda b, t, k: (layer, k, 0)),
            pl.BlockSpec((1, d), lambda b, t, k: (0, 0)),
        ],
        out_specs=pl.BlockSpec((tm, d), lambda b, t, k: (b * nt + t, 0)),
        out_shape=jax.ShapeDtypeStruct((n, d), F32),
        scratch_shapes=[
            pltpu.VMEM((tm, d), BF16),
            pltpu.VMEM((tm + HALO, tf), F32),
            pltpu.VMEM((nk, HALO, tf), F32),
        ],
        compiler_params=pltpu.CompilerParams(
            dimension_semantics=("arbitrary", "arbitrary", "arbitrary"),
            vmem_limit_bytes=VMEM_LIMIT),
        name="convffn",
    )(x, g, wg, wu, conv_w, conv_b, wd, fg)


def _pick(n, pref, step=LANES):
    t = min(n, pref)
    while n % t:
        t -= step
        assert t > 0, (n, pref, step)
    return t


def kernel(x, norm1_g, w_in, dn_conv_w, dn_a_log, dn_dt_bias, dn_onorm_g, sg_ln_g, sg_ln_b, sg_w, sg_b, w_branch_a, w_branch_b, w_out, norm2_g, ffn_w_gate, ffn_w_up, ffn_conv_w, ffn_conv_b, ffn_w_down, final_norm_g):
    batch, seq, d = x.shape
    depth = w_in.shape[0]
    heads = dn_a_log.shape[1]
    groups = sg_w.shape[1]
    width = heads * LANES
    d_ff = ffn_w_gate.shape[2]
    n = batch * seq
    assert dn_onorm_g.shape[1] == LANES and sg_w.shape[2] == SG_CHUNK == LANES
    assert groups * LANES == width and d == 2 * width and 2 * heads <= LANES
    assert dn_conv_w.shape[1] - 1 <= HALO and ffn_conv_w.shape[1] - 1 <= HALO
    ba0 = 4 * width
    assert w_in.shape[2] == 6 * width + 2 * d + 2 * heads

    tm_in = _pick(n, 1024)
    tn_in = _pick(6 * width + 2 * d, 1024)
    tb_dn = _pick(seq, 256)
    tm_mg = _pick(n, 256)
    tm_ff = _pick(seq, 512)
    tf_ff = _pick(d_ff, 512)

    w_main = jnp.concatenate([w_in[:, :, :ba0], w_in[:, :, ba0 + 2 * heads:]], axis=2).astype(BF16)
    w_ba = jnp.pad(w_in[:, :, ba0:ba0 + 2 * heads],
                   ((0, 0), (0, 0), (0, LANES - 2 * heads))).astype(BF16)
    wa16, wb16, wo16 = (w.astype(BF16) for w in (w_branch_a, w_branch_b, w_out))
    wg16, wu16, wd16 = (w.astype(BF16) for w in (ffn_w_gate, ffn_w_up, ffn_w_down))
    norm1 = norm1_g.reshape(depth, 1, d)
    norm2 = norm2_g.reshape(depth, 1, d)
    ln_g = sg_ln_g.reshape(depth, 1, width)
    ln_b = sg_ln_b.reshape(depth, 1, width)
    conv_b = ffn_conv_b.reshape(depth, 1, d_ff)
    fg = final_norm_g.reshape(1, d)

    xf = x.reshape(n, d)
    for l in range(depth):
        proj, ba = _inproj(xf, norm1, w_main, w_ba, l, tm=tm_in, tn=tn_in)

        adt = jnp.zeros((2, LANES), F32)
        adt = adt.at[0, heads:2 * heads].set(dn_a_log[l]).at[1, heads:2 * heads].set(dn_dt_bias[l])
        ya = _deltanet(proj, ba, dn_conv_w[l], adt, dn_onorm_g[l].reshape(1, LANES),
                       batch=batch, seq=seq, heads=heads, tb=tb_dn)

        bias_full = jnp.repeat(jnp.transpose(sg_b[l]), LANES, axis=1)
        xf = _merge(ya, proj, xf, ln_g, ln_b, sg_w, bias_full, wa16, wb16, wo16, l,
                    groups=groups, tm=tm_mg)

        xf = _ffn(xf, norm2, wg16, wu16, ffn_conv_w, conv_b, wd16, fg, l,
                  batch=batch, seq=seq, tm=tm_ff, tf=tf_ff, final_norm=(l == depth - 1))
    return xf.reshape(batch, seq, d)
```

```python
import functools

import jax
import jax.numpy as jnp
from jax import lax
from jax.experimental import pallas as pl
from jax.experimental.pallas import tpu as pltpu

EPS = 1e-6
LANES = 128
DN_CHUNK = 64
SG_CHUNK = 128
HALO = 8
VMEM_LIMIT = 56 * 1024 * 1024

F32 = jnp.float32
BF16 = jnp.bfloat16


def _dot(a, b):
    return jnp.dot(a, b, preferred_element_type=F32)


def _dot_nt(a, b):
    return lax.dot_general(a, b, (((1,), (1,)), ((), ())), preferred_element_type=F32)


def _dot_tn(a, b):
    return lax.dot_general(a, b, (((0,), (0,)), ((), ())), preferred_element_type=F32)


def _split(a):
    hi = a.astype(BF16)
    lo = (a - hi.astype(F32)).astype(BF16)
    return hi, lo


def _dot_split(a_hl, b_hl):
    ah, al = a_hl
    bh, bl = b_hl
    return _dot(ah, bh) + (_dot(ah, bl) + _dot(al, bh))


def _split3(a):
    hi = a.astype(BF16)
    r = a - hi.astype(F32)
    mid = r.astype(BF16)
    lo = (r - mid.astype(F32)).astype(BF16)
    return hi, mid, lo


def _rmsnorm_rows(x_ref, g_ref, o_ref, rows):
    n = x_ref.shape[0] // rows

    def body(i, _):
        r = pl.ds(pl.multiple_of(i * rows, rows), rows)
        x = x_ref[r, :]
        ms = jnp.mean(x * x, axis=-1, keepdims=True)
        o_ref[r, :] = (x * lax.rsqrt(ms + EPS) * g_ref[...]).astype(o_ref.dtype)
        return 0

    lax.fori_loop(0, n, body, 0)


def _inproj_kernel(x_ref, g_ref, w_ref, wba_ref, cw_ref, o_ref, ba_ref,
                   h_ref, carry_ref, *ge_refs, tm, tn, qkv_tiles, tiles_per_seq, rc):
    i = pl.program_id(0)
    j = pl.program_id(1)
    kconv = cw_ref.shape[0]

    @pl.when(j == 0)
    def _():
        _rmsnorm_rows(x_ref, g_ref, h_ref, min(256, tm))
        ba_ref[...] = _dot(h_ref[...], wba_ref[...])

    @pl.when(j >= qkv_tiles)
    def _():
        o_ref[...] = _dot(h_ref[...], w_ref[...]).astype(o_ref.dtype)

    @pl.when(j < qkv_tiles)
    def _():
        @pl.when(i % tiles_per_seq == 0)
        def _():
            carry_ref[j] = jnp.zeros(carry_ref.shape[1:], F32)

        kind = j // (qkv_tiles // 3)
        qscale = jnp.where(kind == 0, float(LANES) ** -0.5, 1.0)
        nchunks = len(ge_refs)
        nslabs = tn // LANES
        for c in range(nchunks):
            res = _dot(h_ref[c * rc:(c + 1) * rc, :], w_ref[...])
            for hd in range(nslabs):
                cols = slice(hd * LANES, (hd + 1) * LANES)
                ge_refs[c][hd, HALO:HALO + rc, :] = res[:, cols]
                if c + 1 < nchunks:
                    ge_refs[c + 1][hd, 0:HALO, :] = res[rc - HALO:rc, cols]
                else:
                    ge_refs[0][hd, 0:HALO, :] = carry_ref[j, hd]
                    carry_ref[j, hd] = res[rc - HALO:rc, cols]
        for c in range(nchunks):
            rows = slice(c * rc, (c + 1) * rc)
            for hd in range(nslabs):
                cols = slice(hd * LANES, (hd + 1) * LANES)
                acc = None
                for t in range(kconv):
                    off = HALO - (kconv - 1) + t
                    term = ge_refs[c][hd, off:off + rc, :] * cw_ref[t:t + 1, cols]
                    acc = term if acc is None else acc + term
                y = acc * jax.nn.sigmoid(acc)
                nrm = lax.rsqrt(jnp.sum(y * y, axis=-1, keepdims=True) + EPS) * qscale
                o_ref[rows, cols] = (y * jnp.where(kind < 2, nrm, 1.0)).astype(o_ref.dtype)


def _inproj(x, g, w, wba, conv_w, layer, *, seq, width, tm, tn):
    n, d = x.shape
    nc = w.shape[2]
    qkv_tiles = 3 * width // tn
    rc = min(128, tm)
    kernel = functools.partial(_inproj_kernel, tm=tm, tn=tn, qkv_tiles=qkv_tiles,
                               tiles_per_seq=seq // tm, rc=rc)
    return pl.pallas_call(
        kernel,
        grid=(n // tm, nc // tn),
        in_specs=[
            pl.BlockSpec((tm, d), lambda i, j: (i, 0)),
            pl.BlockSpec((None, 1, d), lambda i, j: (layer, 0, 0)),
            pl.BlockSpec((None, d, tn), lambda i, j: (layer, 0, j)),
            pl.BlockSpec((None, d, LANES), lambda i, j: (layer, 0, 0)),
            pl.BlockSpec((None, conv_w.shape[1], tn),
                         lambda i, j: (layer, 0, jnp.minimum(j, qkv_tiles - 1))),
        ],
        out_specs=[
            pl.BlockSpec((tm, tn), lambda i, j: (i, j)),
            pl.BlockSpec((tm, LANES), lambda i, j: (i, 0)),
        ],
        out_shape=[
            jax.ShapeDtypeStruct((n, nc), BF16),
            jax.ShapeDtypeStruct((n, LANES), F32),
        ],
        scratch_shapes=[
            pltpu.VMEM((tm, d), BF16),
            pltpu.VMEM((qkv_tiles, tn // LANES, HALO, LANES), F32),
        ] + [pltpu.VMEM((tn // LANES, rc + HALO, LANES), F32)] * (tm // rc),
        compiler_params=pltpu.CompilerParams(
            dimension_semantics=("arbitrary", "arbitrary"),
            vmem_limit_bytes=VMEM_LIMIT),
        name="inproj",
    )(x, g, w, wba, conv_w)


def _deltanet_kernel(qkv_ref, z_ref, ba_ref, adt_ref, og_ref, y_ref,
                     gb_s, gt_s, u_s, lhs_s, attn_s, kdec_s, s_ref, *, heads, tb):
    C = DN_CHUNK

    @pl.when(pl.program_id(1) == 0)
    def _():
        s_ref[...] = jnp.zeros(s_ref.shape, F32)

    ba = ba_ref[...]
    beta_all = jax.nn.sigmoid(ba)
    xa = ba + adt_ref[1:2, :]
    softplus = jnp.maximum(xa, 0.0) + jnp.log1p(jnp.exp(-jnp.abs(xa)))
    g_all = -jnp.exp(adt_ref[0:1, :]) * softplus
    trow = lax.broadcasted_iota(jnp.int32, (tb, tb), 0)
    tcol = lax.broadcasted_iota(jnp.int32, (tb, tb), 1)
    chunk_tril = jnp.where(((trow // C) == (tcol // C)) & (trow >= tcol), 1.0, 0.0).astype(BF16)
    g3 = _split3(g_all)
    gc_all = _dot(chunk_tril, g3[0]) + (_dot(chunk_tril, g3[1]) + _dot(chunk_tril, g3[2]))
    gt_s[...] = gc_all.T
    lane = lax.broadcasted_iota(jnp.int32, (tb, LANES), 1)
    for h in range(heads):
        bcol = jnp.sum(jnp.where(lane == h, beta_all, 0.0), axis=-1, keepdims=True)
        gcol = jnp.sum(jnp.where(lane == heads + h, gc_all, 0.0), axis=-1, keepdims=True)
        gb_s[h] = jnp.broadcast_to(bcol, (tb, LANES))
        gb_s[heads + h] = jnp.broadcast_to(gcol, (tb, LANES))

    row = lax.broadcasted_iota(jnp.int32, (C, C), 0)
    col = lax.broadcasted_iota(jnp.int32, (C, C), 1)
    incl = row >= col
    strict = row > col
    eye = jnp.where(row == col, 1.0, 0.0).astype(F32)
    sizes = []
    s = 1
    while s < C:
        sizes.append(s)
        s *= 2
    level_masks = [
        ((row // (2 * s)) == (col // (2 * s))) & ((row & s) != 0) & ((col & s) == 0)
        for s in sizes
    ]
    nchunks = tb // C
    problems = [(c, h) for c in range(nchunks) for h in range(heads)]

    lmats = []
    for c, h in problems:
        rows = slice(c * C, (c + 1) * C)
        q = qkv_ref[rows, h * LANES:(h + 1) * LANES].astype(F32)
        k = qkv_ref[rows, (heads + h) * LANES:(heads + h + 1) * LANES].astype(F32)
        beta = gb_s[h, rows, :]
        gcol = gb_s[heads + h, rows, :]
        grow = gt_s[heads + h:heads + h + 1, rows]
        decay = jnp.where(incl, jnp.exp(gcol[:, :C] - grow), 0.0)
        kq = _dot_nt(jnp.concatenate([k * beta, q], axis=0).astype(BF16), k.astype(BF16))
        lmats.append(jnp.where(strict, kq[:C] * decay, 0.0))
        attn_s[h, rows, :] = (kq[C:] * decay).astype(BF16)

    tinvs = [eye - jnp.where(level_masks[0], lm, 0.0) for lm in lmats]
    for m in level_masks[1:]:
        t16 = [t.astype(BF16) for t in tinvs]
        tc = [_dot(t, jnp.where(m, lm, 0.0).astype(BF16)) for t, lm in zip(t16, lmats)]
        tinvs = [t - _dot(x.astype(BF16), t_b) for t, x, t_b in zip(tinvs, tc, t16)]

    for (c, h), tinv in zip(problems, tinvs):
        rows = slice(c * C, (c + 1) * C)
        q = qkv_ref[rows, h * LANES:(h + 1) * LANES].astype(F32)
        k = qkv_ref[rows, (heads + h) * LANES:(heads + h + 1) * LANES].astype(F32)
        v = qkv_ref[rows, (2 * heads + h) * LANES:(2 * heads + h + 1) * LANES].astype(F32)
        beta = gb_s[h, rows, :]
        gcol = gb_s[heads + h, rows, :]
        egc = jnp.exp(gcol)
        rhs = jnp.concatenate([v * beta, k * beta * egc], axis=1).astype(BF16)
        uw = _dot(tinv.astype(BF16), rhs)
        u_s[h, rows, :] = uw[:, :LANES]
        lhs_s[h, 2 * c * C:(2 * c + 1) * C, :] = uw[:, LANES:].astype(BF16)
        lhs_s[h, (2 * c + 1) * C:(2 * c + 2) * C, :] = (q * egc).astype(BF16)
        g_last = gcol[C - 1:C, :]
        kdec_s[h, rows, :] = (k * jnp.exp(g_last - gcol)).astype(BF16)

    for c in range(nchunks):
        rows = slice(c * C, (c + 1) * C)
        s_old = [s_ref[h] for h in range(heads)]
        wq = [_dot(lhs_s[h, 2 * c * C:(2 * c + 2) * C, :], s_old[h].astype(BF16))
              for h in range(heads)]
        v16 = [(u_s[h, rows, :] - wq[h][:C]).astype(BF16) for h in range(heads)]
        for h in range(heads):
            g_last = gb_s[heads + h, (c + 1) * C - 1:(c + 1) * C, :]
            s_ref[h] = s_old[h] * jnp.exp(g_last) + _dot_tn(kdec_s[h, rows, :], v16[h])
        for h in range(heads):
            o = wq[h][C:] + _dot(attn_s[h, rows, :], v16[h])
            zz = z_ref[rows, h * LANES:(h + 1) * LANES].astype(F32)
            on = o * lax.rsqrt(jnp.mean(o * o, axis=-1, keepdims=True) + EPS)
            y = on * og_ref[...] * (zz * jax.nn.sigmoid(zz))
            y_ref[rows, h * LANES:(h + 1) * LANES] = y.astype(y_ref.dtype)


def _deltanet(proj, ba, adt, onorm_g, *, batch, seq, heads, tb):
    n = proj.shape[0]
    width = heads * LANES
    nt = seq // tb
    kernel = functools.partial(_deltanet_kernel, heads=heads, tb=tb)
    return pl.pallas_call(
        kernel,
        grid=(batch, nt),
        in_specs=[
            pl.BlockSpec((tb, 3 * width), lambda b, t: (b * nt + t, 0)),
            pl.BlockSpec((tb, width), lambda b, t: (b * nt + t, 3)),
            pl.BlockSpec((tb, LANES), lambda b, t: (b * nt + t, 0)),
            pl.BlockSpec(adt.shape, lambda b, t: (0, 0)),
            pl.BlockSpec((1, LANES), lambda b, t: (0, 0)),
        ],
        out_specs=pl.BlockSpec((tb, width), lambda b, t: (b * nt + t, 0)),
        out_shape=jax.ShapeDtypeStruct((n, width), BF16),
        scratch_shapes=[
            pltpu.VMEM((2 * heads, tb, LANES), F32),
            pltpu.VMEM((LANES, tb), F32),
            pltpu.VMEM((heads, tb, LANES), F32),
            pltpu.VMEM((heads, 2 * tb, LANES), BF16),
            pltpu.VMEM((heads, tb, DN_CHUNK), BF16),
            pltpu.VMEM((heads, tb, LANES), BF16),
            pltpu.VMEM((heads, LANES, LANES), F32),
        ],
        compiler_params=pltpu.CompilerParams(
            dimension_semantics=("arbitrary", "arbitrary"),
            vmem_limit_bytes=VMEM_LIMIT),
        name="deltanet",
    )(proj, proj, ba, adt, onorm_g)


def _gelu(x):
    return 0.5 * x * (1.0 + lax.erf(x * (0.5 ** 0.5)))


def _merge_kernel(ya_ref, u_ref, v_ref, ga_ref, gb_ref, x_ref, lg_ref, lb_ref, sw_ref, bias_ref,
                  wa_ref, wb_ref, wo_ref, o_ref, yb_s, *, groups, tm):
    C = SG_CHUNK
    merged_a = jax.nn.sigmoid(ga_ref[...].astype(F32)) * _dot(ya_ref[...], wa_ref[...])

    v = _gelu(v_ref[...].astype(F32))
    mu = jnp.mean(v, axis=-1, keepdims=True)
    vc = v - mu
    var = jnp.mean(vc * vc, axis=-1, keepdims=True)
    vg = (vc * lax.rsqrt(var + EPS) * lg_ref[...] + lb_ref[...]).astype(BF16)
    row = lax.broadcasted_iota(jnp.int32, (C, C), 0)
    col = lax.broadcasted_iota(jnp.int32, (C, C), 1)
    causal = row >= col
    for g in range(groups):
        cols = slice(g * LANES, (g + 1) * LANES)
        wg = jnp.where(causal, sw_ref[g], 0.0).astype(BF16)
        for c in range(tm // C):
            rows = slice(c * C, (c + 1) * C)
            mixed = _dot(wg, vg[rows, cols]) + bias_ref[:, cols]
            yb_s[rows, cols] = (_gelu(u_ref[rows, cols].astype(F32)) * mixed).astype(BF16)

    mb = _dot(yb_s[...], wb_ref[...])
    merged = merged_a + jax.nn.sigmoid(gb_ref[...].astype(F32)) * mb
    o_ref[...] = x_ref[...] + _dot(merged.astype(BF16), wo_ref[...])


def _merge(ya, proj, x, ln_g, ln_b, sg_w, bias_full, wa, wb, wo, layer, *, groups, tm):
    n, d = x.shape
    width = ya.shape[1]
    gate_block = (6 * width) // d
    const = dict(pipeline_mode=pl.Buffered(1))
    kernel = functools.partial(_merge_kernel, groups=groups, tm=tm)
    return pl.pallas_call(
        kernel,
        grid=(n // tm,),
        in_specs=[
            pl.BlockSpec((tm, width), lambda i: (i, 0)),
            pl.BlockSpec((tm, width), lambda i: (i, 4)),
            pl.BlockSpec((tm, width), lambda i: (i, 5)),
            pl.BlockSpec((tm, d), lambda i: (i, gate_block)),
            pl.BlockSpec((tm, d), lambda i: (i, gate_block + 1)),
            pl.BlockSpec((tm, d), lambda i: (i, 0)),
            pl.BlockSpec((None, 1, width), lambda i: (layer, 0, 0)),
            pl.BlockSpec((None, 1, width), lambda i: (layer, 0, 0)),
            pl.BlockSpec((None,) + sg_w.shape[1:], lambda i: (layer, 0, 0, 0)),
            pl.BlockSpec(bias_full.shape, lambda i: (0, 0)),
            pl.BlockSpec((None,) + wa.shape[1:], lambda i: (layer, 0, 0), **const),
            pl.BlockSpec((None,) + wb.shape[1:], lambda i: (layer, 0, 0), **const),
            pl.BlockSpec((None,) + wo.shape[1:], lambda i: (layer, 0, 0), **const),
        ],
        out_specs=pl.BlockSpec((tm, d), lambda i: (i, 0)),
        out_shape=jax.ShapeDtypeStruct((n, d), F32),
        scratch_shapes=[pltpu.VMEM((tm, width), BF16)],
        compiler_params=pltpu.CompilerParams(
            dimension_semantics=("arbitrary",),
            vmem_limit_bytes=VMEM_LIMIT),
        name="gmlp_merge_out",
    )(ya, proj, proj, proj, proj, x, ln_g, ln_b, sg_w, bias_full, wa, wb, wo)


def _ffn_kernel(x_ref, g_ref, wg_ref, wu_ref, cw_ref, cb_ref, wd_ref, fg_ref, o_ref,
                h_ref, ge_ref, act_ref, carry_ref, *, tm, final_norm):
    t = pl.program_id(1)
    k = pl.program_id(2)
    nk = pl.num_programs(2)
    kconv = cw_ref.shape[0]

    @pl.when(k == 0)
    def _():
        _rmsnorm_rows(x_ref, g_ref, h_ref, min(256, tm))
        o_ref[...] = x_ref[...]

    @pl.when(t == 0)
    def _():
        carry_ref[k] = jnp.zeros(carry_ref.shape[1:], F32)

    h = h_ref[...]
    gp = _dot(h, wg_ref[...])
    up = _dot(h, wu_ref[...])
    for sl in range(ge_ref.shape[0]):
        cols = slice(sl * LANES, (sl + 1) * LANES)
        ge_ref[sl, 0:HALO, :] = carry_ref[k, sl]
        ge_ref[sl, HALO:HALO + tm, :] = gp[:, cols]
        carry_ref[k, sl] = gp[tm - HALO:tm, cols]
        acc = cb_ref[:, cols]
        for j in range(kconv):
            off = HALO - (kconv - 1) + j
            acc = acc + ge_ref[sl, off:off + tm, :] * cw_ref[j:j + 1, cols]
        act_ref[:, cols] = (acc * jax.nn.sigmoid(acc) * up[:, cols]).astype(BF16)
    o_ref[...] += _dot(act_ref[...], wd_ref[...])

    if final_norm:
        @pl.when(k == nk - 1)
        def _():
            _rmsnorm_rows(o_ref, fg_ref, o_ref, min(256, tm))


def _ffn(x, g, wg, wu, conv_w, conv_b, wd, fg, layer, *, batch, seq, tm, tf, final_norm):
    n, d = x.shape
    f = wg.shape[2]
    nt = seq // tm
    nk = f // tf
    kernel = functools.partial(_ffn_kernel, tm=tm, final_norm=final_norm)
    return pl.pallas_call(
        kernel,
        grid=(batch, nt, nk),
        in_specs=[
            pl.BlockSpec((tm, d), lambda b, t, k: (b * nt + t, 0)),
            pl.BlockSpec((None, 1, d), lambda b, t, k: (layer, 0, 0)),
            pl.BlockSpec((None, d, tf), lambda b, t, k: (layer, 0, k)),
            pl.BlockSpec((None, d, tf), lambda b, t, k: (layer, 0, k)),
            pl.BlockSpec((None, conv_w.shape[1], tf), lambda b, t, k: (layer, 0, k)),
            pl.BlockSpec((None, 1, tf), lambda b, t, k: (layer, 0, k)),
            pl.BlockSpec((None, tf, d), lambda b, t, k: (layer, k, 0)),
            pl.BlockSpec((1, d), lambda b, t, k: (0, 0)),
        ],
        out_specs=pl.BlockSpec((tm, d), lambda b, t, k: (b * nt + t, 0)),
        out_shape=jax.ShapeDtypeStruct((n, d), F32),
        scratch_shapes=[
            pltpu.VMEM((tm, d), BF16),
            pltpu.VMEM((tf // LANES, tm + HALO, LANES), F32),
            pltpu.VMEM((tm, tf), BF16),
            pltpu.VMEM((nk, tf // LANES, HALO, LANES), F32),
        ],
        compiler_params=pltpu.CompilerParams(
            dimension_semantics=("arbitrary", "arbitrary", "arbitrary"),
            vmem_limit_bytes=VMEM_LIMIT),
        name="convffn",
    )(x, g, wg, wu, conv_w, conv_b, wd, fg)


def _pick(n, pref, step=LANES):
    t = min(n, pref)
    while n % t:
        t -= step
        assert t > 0, (n, pref, step)
    return t


def kernel(x, norm1_g, w_in, dn_conv_w, dn_a_log, dn_dt_bias, dn_onorm_g, sg_ln_g, sg_ln_b, sg_w, sg_b, w_branch_a, w_branch_b, w_out, norm2_g, ffn_w_gate, ffn_w_up, ffn_conv_w, ffn_conv_b, ffn_w_down, final_norm_g):
    batch, seq, d = x.shape
    depth = w_in.shape[0]
    heads = dn_a_log.shape[1]
    groups = sg_w.shape[1]
    width = heads * LANES
    d_ff = ffn_w_gate.shape[2]
    n = batch * seq
    assert dn_onorm_g.shape[1] == LANES and sg_w.shape[2] == SG_CHUNK == LANES
    assert groups * LANES == width and d == 2 * width and 2 * heads <= LANES
    assert dn_conv_w.shape[1] - 1 <= HALO and ffn_conv_w.shape[1] - 1 <= HALO
    ba0 = 4 * width
    assert w_in.shape[2] == 6 * width + 2 * d + 2 * heads

    tm_in = _pick(seq, 1024)
    tn_in = _pick(width, 1024)
    tb_dn = _pick(seq, 256)
    tm_mg = _pick(n, 256)
    tm_ff = _pick(seq, 512)
    tf_ff = _pick(d_ff, 512)

    w_main = jnp.concatenate([w_in[:, :, :ba0], w_in[:, :, ba0 + 2 * heads:]], axis=2).astype(BF16)
    w_ba = jnp.pad(w_in[:, :, ba0:ba0 + 2 * heads],
                   ((0, 0), (0, 0), (0, LANES - 2 * heads))).astype(BF16)
    wa16, wb16, wo16 = (w.astype(BF16) for w in (w_branch_a, w_branch_b, w_out))
    wg16, wu16, wd16 = (w.astype(BF16) for w in (ffn_w_gate, ffn_w_up, ffn_w_down))
    norm1 = norm1_g.reshape(depth, 1, d)
    norm2 = norm2_g.reshape(depth, 1, d)
    ln_g = sg_ln_g.reshape(depth, 1, width)
    ln_b = sg_ln_b.reshape(depth, 1, width)
    conv_b = ffn_conv_b.reshape(depth, 1, d_ff)
    fg = final_norm_g.reshape(1, d)

    xf = x.reshape(n, d)
    for l in range(depth):
        proj, ba = _inproj(xf, norm1, w_main, w_ba, dn_conv_w, l,
                           seq=seq, width=width, tm=tm_in, tn=tn_in)

        adt = jnp.zeros((2, LANES), F32)
        adt = adt.at[0, heads:2 * heads].set(dn_a_log[l]).at[1, heads:2 * heads].set(dn_dt_bias[l])
        ya = _deltanet(proj, ba, adt, dn_onorm_g[l].reshape(1, LANES),
                       batch=batch, seq=seq, heads=heads, tb=tb_dn)

        bias_full = jnp.repeat(jnp.transpose(sg_b[l]), LANES, axis=1)
        xf = _merge(ya, proj, xf, ln_g, ln_b, sg_w, bias_full, wa16, wb16, wo16, l,
                    groups=groups, tm=tm_mg)

        xf = _ffn(xf, norm2, wg16, wu16, ffn_conv_w, conv_b, wd16, fg, l,
                  batch=batch, seq=seq, tm=tm_ff, tf=tf_ff, final_norm=(l == depth - 1))
    return xf.reshape(batch, seq, d)
```

```python
import functools

import jax
import jax.numpy as jnp
from jax import lax
from jax.experimental import pallas as pl
from jax.experimental.pallas import tpu as pltpu

EPS = 1e-6
LANES = 128
DN_CHUNK = 64
SG_CHUNK = 128
HALO = 8
VMEM_LIMIT = 60 * 1024 * 1024

F32 = jnp.float32
BF16 = jnp.bfloat16


def _dot(a, b):
    return jnp.dot(a, b, preferred_element_type=F32)


def _dot_nt(a, b):
    return lax.dot_general(a, b, (((1,), (1,)), ((), ())), preferred_element_type=F32)


def _dot_tn(a, b):
    return lax.dot_general(a, b, (((0,), (0,)), ((), ())), preferred_element_type=F32)


def _split(a):
    hi = a.astype(BF16)
    lo = (a - hi.astype(F32)).astype(BF16)
    return hi, lo


def _dot_split(a_hl, b_hl):
    ah, al = a_hl
    bh, bl = b_hl
    return _dot(ah, bh) + (_dot(ah, bl) + _dot(al, bh))


def _split3(a):
    hi = a.astype(BF16)
    r = a - hi.astype(F32)
    mid = r.astype(BF16)
    lo = (r - mid.astype(F32)).astype(BF16)
    return hi, mid, lo


def _rmsnorm_rows(x_ref, g_ref, o_ref, rows):
    n = x_ref.shape[0] // rows

    def body(i, _):
        r = pl.ds(pl.multiple_of(i * rows, rows), rows)
        x = x_ref[r, :]
        ms = jnp.mean(x * x, axis=-1, keepdims=True)
        o_ref[r, :] = (x * lax.rsqrt(ms + EPS) * g_ref[...]).astype(o_ref.dtype)
        return 0

    lax.fori_loop(0, n, body, 0)


def _inproj_kernel(x_ref, g_ref, w1_ref, w2_ref, wba_ref, cw_ref, o_ref, ba_ref,
                   h_ref, carry_ref, *ge_refs, tm, tn, qkv_tiles, w1_tiles, tiles_per_seq, rc):
    i = pl.program_id(0)
    j = pl.program_id(1)
    kconv = cw_ref.shape[0]

    @pl.when(j == 0)
    def _():
        _rmsnorm_rows(x_ref, g_ref, h_ref, min(256, tm))
        ba_ref[...] = _dot(h_ref[...], wba_ref[...])

    @pl.when((j >= qkv_tiles) & (j < w1_tiles))
    def _():
        o_ref[...] = _dot(h_ref[...], w1_ref[...]).astype(o_ref.dtype)

    @pl.when(j >= w1_tiles)
    def _():
        o_ref[...] = _dot(h_ref[...], w2_ref[...]).astype(o_ref.dtype)

    @pl.when(j < qkv_tiles)
    def _():
        @pl.when(i % tiles_per_seq == 0)
        def _():
            carry_ref[j] = jnp.zeros(carry_ref.shape[1:], F32)

        kind = j // (qkv_tiles // 3)
        qscale = jnp.where(kind == 0, float(LANES) ** -0.5, 1.0)
        nchunks = len(ge_refs)
        nslabs = tn // LANES
        for c in range(nchunks):
            res = _dot(h_ref[c * rc:(c + 1) * rc, :], w1_ref[...])
            for hd in range(nslabs):
                cols = slice(hd * LANES, (hd + 1) * LANES)
                ge_refs[c][hd, HALO:HALO + rc, :] = res[:, cols]
                if c + 1 < nchunks:
                    ge_refs[c + 1][hd, 0:HALO, :] = res[rc - HALO:rc, cols]
                else:
                    ge_refs[0][hd, 0:HALO, :] = carry_ref[j, hd]
                    carry_ref[j, hd] = res[rc - HALO:rc, cols]
        for c in range(nchunks):
            rows = slice(c * rc, (c + 1) * rc)
            for hd in range(nslabs):
                cols = slice(hd * LANES, (hd + 1) * LANES)
                acc = None
                for t in range(kconv):
                    off = HALO - (kconv - 1) + t
                    term = ge_refs[c][hd, off:off + rc, :] * cw_ref[t:t + 1, cols]
                    acc = term if acc is None else acc + term
                y = acc * jax.nn.sigmoid(acc)
                nrm = lax.rsqrt(jnp.sum(y * y, axis=-1, keepdims=True) + EPS) * qscale
                o_ref[rows, cols] = (y * jnp.where(kind < 2, nrm, 1.0)).astype(o_ref.dtype)


def _inproj(x, g, w1, w2, wba, conv_w, layer, *, seq, width, tm, tn):
    n, d = x.shape
    nc = w1.shape[2] + w2.shape[2]
    qkv_tiles = 3 * width // tn
    w1_tiles = w1.shape[2] // tn
    rc = min(128, tm)
    kernel = functools.partial(_inproj_kernel, tm=tm, tn=tn, qkv_tiles=qkv_tiles,
                               w1_tiles=w1_tiles, tiles_per_seq=seq // tm, rc=rc)
    return pl.pallas_call(
        kernel,
        grid=(n // tm, nc // tn),
        in_specs=[
            pl.BlockSpec((tm, d), lambda i, j: (i, 0)),
            pl.BlockSpec((None, 1, d), lambda i, j: (layer, 0, 0)),
            pl.BlockSpec((None, d, tn), lambda i, j: (layer, 0, jnp.minimum(j, w1_tiles - 1))),
            pl.BlockSpec((None, d, tn), lambda i, j: (layer, 0, jnp.maximum(j - w1_tiles, 0))),
            pl.BlockSpec((None, d, LANES), lambda i, j: (layer, 0, 0)),
            pl.BlockSpec((None, conv_w.shape[1], tn),
                         lambda i, j: (layer, 0, jnp.minimum(j, qkv_tiles - 1))),
        ],
        out_specs=[
            pl.BlockSpec((tm, tn), lambda i, j: (i, j)),
            pl.BlockSpec((tm, LANES), lambda i, j: (i, 0)),
        ],
        out_shape=[
            jax.ShapeDtypeStruct((n, nc), BF16),
            jax.ShapeDtypeStruct((n, LANES), F32),
        ],
        scratch_shapes=[
            pltpu.VMEM((tm, d), BF16),
            pltpu.VMEM((qkv_tiles, tn // LANES, HALO, LANES), F32),
        ] + [pltpu.VMEM((tn // LANES, rc + HALO, LANES), F32)] * (tm // rc),
        compiler_params=pltpu.CompilerParams(
            dimension_semantics=("arbitrary", "arbitrary"),
            vmem_limit_bytes=VMEM_LIMIT),
        name="inproj",
    )(x, g, w1, w2, wba, conv_w)


def _deltanet_kernel(qkv_ref, z_ref, ba_ref, adt_ref, og_ref, y_ref,
                     gb_s, gt_s, u_s, lhs_s, attn_s, kdec_s, s_ref, *, heads, tb):
    C = DN_CHUNK

    @pl.when(pl.program_id(1) == 0)
    def _():
        s_ref[...] = jnp.zeros(s_ref.shape, F32)

    ba = ba_ref[...]
    beta_all = jax.nn.sigmoid(ba)
    xa = ba + adt_ref[1:2, :]
    softplus = jnp.maximum(xa, 0.0) + jnp.log1p(jnp.exp(-jnp.abs(xa)))
    g_all = -jnp.exp(adt_ref[0:1, :]) * softplus
    trow = lax.broadcasted_iota(jnp.int32, (tb, tb), 0)
    tcol = lax.broadcasted_iota(jnp.int32, (tb, tb), 1)
    chunk_tril = jnp.where(((trow // C) == (tcol // C)) & (trow >= tcol), 1.0, 0.0).astype(BF16)
    g3 = _split3(g_all)
    gc_all = _dot(chunk_tril, g3[0]) + (_dot(chunk_tril, g3[1]) + _dot(chunk_tril, g3[2]))
    gt_s[...] = gc_all.T
    lane = lax.broadcasted_iota(jnp.int32, (tb, LANES), 1)
    for h in range(heads):
        bcol = jnp.sum(jnp.where(lane == h, beta_all, 0.0), axis=-1, keepdims=True)
        gcol = jnp.sum(jnp.where(lane == heads + h, gc_all, 0.0), axis=-1, keepdims=True)
        gb_s[h] = jnp.broadcast_to(bcol, (tb, LANES))
        gb_s[heads + h] = jnp.broadcast_to(gcol, (tb, LANES))

    row = lax.broadcasted_iota(jnp.int32, (C, C), 0)
    col = lax.broadcasted_iota(jnp.int32, (C, C), 1)
    incl = row >= col
    strict = row > col
    eye = jnp.where(row == col, 1.0, 0.0).astype(F32)
    sizes = []
    s = 1
    while s < C:
        sizes.append(s)
        s *= 2
    level_masks = [
        ((row // (2 * s)) == (col // (2 * s))) & ((row & s) != 0) & ((col & s) == 0)
        for s in sizes
    ]
    nchunks = tb // C
    problems = [(c, h) for c in range(nchunks) for h in range(heads)]

    lmats = []
    for c, h in problems:
        rows = slice(c * C, (c + 1) * C)
        q = qkv_ref[rows, h * LANES:(h + 1) * LANES].astype(F32)
        k = qkv_ref[rows, (heads + h) * LANES:(heads + h + 1) * LANES].astype(F32)
        beta = gb_s[h, rows, :]
        gcol = gb_s[heads + h, rows, :]
        grow = gt_s[heads + h:heads + h + 1, rows]
        decay = jnp.where(incl, jnp.exp(gcol[:, :C] - grow), 0.0)
        kq = _dot_nt(jnp.concatenate([k * beta, q], axis=0).astype(BF16), k.astype(BF16))
        lmats.append(jnp.where(strict, kq[:C] * decay, 0.0))
        attn_s[h, rows, :] = (kq[C:] * decay).astype(BF16)

    tinvs = [eye - jnp.where(level_masks[0], lm, 0.0) for lm in lmats]
    for m in level_masks[1:]:
        t16 = [t.astype(BF16) for t in tinvs]
        tc = [_dot(t, jnp.where(m, lm, 0.0).astype(BF16)) for t, lm in zip(t16, lmats)]
        tinvs = [t - _dot(x.astype(BF16), t_b) for t, x, t_b in zip(tinvs, tc, t16)]

    for (c, h), tinv in zip(problems, tinvs):
        rows = slice(c * C, (c + 1) * C)
        q = qkv_ref[rows, h * LANES:(h + 1) * LANES].astype(F32)
        k = qkv_ref[rows, (heads + h) * LANES:(heads + h + 1) * LANES].astype(F32)
        v = qkv_ref[rows, (2 * heads + h) * LANES:(2 * heads + h + 1) * LANES].astype(F32)
        beta = gb_s[h, rows, :]
        gcol = gb_s[heads + h, rows, :]
        egc = jnp.exp(gcol)
        rhs = jnp.concatenate([v * beta, k * beta * egc], axis=1).astype(BF16)
        uw = _dot(tinv.astype(BF16), rhs)
        u_s[h, rows, :] = uw[:, :LANES]
        lhs_s[h, 2 * c * C:(2 * c + 1) * C, :] = uw[:, LANES:].astype(BF16)
        lhs_s[h, (2 * c + 1) * C:(2 * c + 2) * C, :] = (q * egc).astype(BF16)
        g_last = gcol[C - 1:C, :]
        kdec_s[h, rows, :] = (k * jnp.exp(g_last - gcol)).astype(BF16)

    for c in range(nchunks):
        rows = slice(c * C, (c + 1) * C)
        s_old = [s_ref[h] for h in range(heads)]
        wq = [_dot(lhs_s[h, 2 * c * C:(2 * c + 2) * C, :], s_old[h].astype(BF16))
              for h in range(heads)]
        v16 = [(u_s[h, rows, :] - wq[h][:C]).astype(BF16) for h in range(heads)]
        for h in range(heads):
            g_last = gb_s[heads + h, (c + 1) * C - 1:(c + 1) * C, :]
            s_ref[h] = s_old[h] * jnp.exp(g_last) + _dot_tn(kdec_s[h, rows, :], v16[h])
        for h in range(heads):
            o = wq[h][C:] + _dot(attn_s[h, rows, :], v16[h])
            zz = z_ref[rows, h * LANES:(h + 1) * LANES].astype(F32)
            on = o * lax.rsqrt(jnp.mean(o * o, axis=-1, keepdims=True) + EPS)
            y = on * og_ref[...] * (zz * jax.nn.sigmoid(zz))
            y_ref[rows, h * LANES:(h + 1) * LANES] = y.astype(y_ref.dtype)


def _deltanet(proj, ba, adt, onorm_g, *, batch, seq, heads, tb):
    n = proj.shape[0]
    width = heads * LANES
    nt = seq // tb
    kernel = functools.partial(_deltanet_kernel, heads=heads, tb=tb)
    return pl.pallas_call(
        kernel,
        grid=(batch, nt),
        in_specs=[
            pl.BlockSpec((tb, 3 * width), lambda b, t: (b * nt + t, 0)),
            pl.BlockSpec((tb, width), lambda b, t: (b * nt + t, 3)),
            pl.BlockSpec((tb, LANES), lambda b, t: (b * nt + t, 0)),
            pl.BlockSpec(adt.shape, lambda b, t: (0, 0)),
            pl.BlockSpec((1, LANES), lambda b, t: (0, 0)),
        ],
        out_specs=pl.BlockSpec((tb, width), lambda b, t: (b * nt + t, 0)),
        out_shape=jax.ShapeDtypeStruct((n, width), BF16),
        scratch_shapes=[
            pltpu.VMEM((2 * heads, tb, LANES), F32),
            pltpu.VMEM((LANES, tb), F32),
            pltpu.VMEM((heads, tb, LANES), F32),
            pltpu.VMEM((heads, 2 * tb, LANES), BF16),
            pltpu.VMEM((heads, tb, DN_CHUNK), BF16),
            pltpu.VMEM((heads, tb, LANES), BF16),
            pltpu.VMEM((heads, LANES, LANES), F32),
        ],
        compiler_params=pltpu.CompilerParams(
            dimension_semantics=("arbitrary", "arbitrary"),
            vmem_limit_bytes=VMEM_LIMIT),
        name="deltanet",
    )(proj, proj, ba, adt, onorm_g)


def _gelu(x):
    return 0.5 * x * (1.0 + lax.erf(x * (0.5 ** 0.5)))


def _merge_kernel(ya_ref, u_ref, v_ref, ga_ref, gb_ref, x_ref, lg_ref, lb_ref, sw_ref, bias_ref,
                  wa_ref, wb_ref, wo_ref, o_ref, yb_s, *, groups, tm):
    C = SG_CHUNK
    merged_a = jax.nn.sigmoid(ga_ref[...].astype(F32)) * _dot(ya_ref[...], wa_ref[...])

    v = _gelu(v_ref[...].astype(F32))
    mu = jnp.mean(v, axis=-1, keepdims=True)
    vc = v - mu
    var = jnp.mean(vc * vc, axis=-1, keepdims=True)
    vg = (vc * lax.rsqrt(var + EPS) * lg_ref[...] + lb_ref[...]).astype(BF16)
    row = lax.broadcasted_iota(jnp.int32, (C, C), 0)
    col = lax.broadcasted_iota(jnp.int32, (C, C), 1)
    causal = row >= col
    for g in range(groups):
        cols = slice(g * LANES, (g + 1) * LANES)
        wg = jnp.where(causal, sw_ref[g], 0.0).astype(BF16)
        for c in range(tm // C):
            rows = slice(c * C, (c + 1) * C)
            mixed = _dot(wg, vg[rows, cols]) + bias_ref[:, cols]
            yb_s[rows, cols] = (_gelu(u_ref[rows, cols].astype(F32)) * mixed).astype(BF16)

    mb = _dot(yb_s[...], wb_ref[...])
    merged = merged_a + jax.nn.sigmoid(gb_ref[...].astype(F32)) * mb
    o_ref[...] = x_ref[...] + _dot(merged.astype(BF16), wo_ref[...])


def _merge(ya, proj, x, ln_g, ln_b, sg_w, bias_full, wa, wb, wo, layer, *, groups, tm):
    n, d = x.shape
    width = ya.shape[1]
    gate_block = (6 * width) // d
    const = dict(pipeline_mode=pl.Buffered(1))
    kernel = functools.partial(_merge_kernel, groups=groups, tm=tm)
    return pl.pallas_call(
        kernel,
        grid=(n // tm,),
        in_specs=[
            pl.BlockSpec((tm, width), lambda i: (i, 0)),
            pl.BlockSpec((tm, width), lambda i: (i, 4)),
            pl.BlockSpec((tm, width), lambda i: (i, 5)),
            pl.BlockSpec((tm, d), lambda i: (i, gate_block)),
            pl.BlockSpec((tm, d), lambda i: (i, gate_block + 1)),
            pl.BlockSpec((tm, d), lambda i: (i, 0)),
            pl.BlockSpec((None, 1, width), lambda i: (layer, 0, 0)),
            pl.BlockSpec((None, 1, width), lambda i: (layer, 0, 0)),
            pl.BlockSpec((None,) + sg_w.shape[1:], lambda i: (layer, 0, 0, 0)),
            pl.BlockSpec(bias_full.shape, lambda i: (0, 0)),
            pl.BlockSpec((None,) + wa.shape[1:], lambda i: (layer, 0, 0), **const),
            pl.BlockSpec((None,) + wb.shape[1:], lambda i: (layer, 0, 0), **const),
            pl.BlockSpec((None,) + wo.shape[1:], lambda i: (layer, 0, 0), **const),
        ],
        out_specs=pl.BlockSpec((tm, d), lambda i: (i, 0)),
        out_shape=jax.ShapeDtypeStruct((n, d), F32),
        scratch_shapes=[pltpu.VMEM((tm, width), BF16)],
        compiler_params=pltpu.CompilerParams(
            dimension_semantics=("arbitrary",),
            vmem_limit_bytes=VMEM_LIMIT),
        name="gmlp_merge_out",
    )(ya, proj, proj, proj, proj, x, ln_g, ln_b, sg_w, bias_full, wa, wb, wo)


def _ffn_kernel(x_ref, g_ref, wg_ref, wu_ref, cw_ref, cb_ref, wd_ref, fg_ref, o_ref,
                h_ref, ge_ref, act_ref, carry_ref, *, tm, final_norm):
    t = pl.program_id(1)
    k = pl.program_id(2)
    nk = pl.num_programs(2)
    kconv = cw_ref.shape[0]

    @pl.when(k == 0)
    def _():
        _rmsnorm_rows(x_ref, g_ref, h_ref, min(256, tm))
        o_ref[...] = x_ref[...]

    @pl.when(t == 0)
    def _():
        carry_ref[k] = jnp.zeros(carry_ref.shape[1:], F32)

    h = h_ref[...]
    gp = _dot(h, wg_ref[...])
    up = _dot(h, wu_ref[...])
    for sl in range(ge_ref.shape[0]):
        cols = slice(sl * LANES, (sl + 1) * LANES)
        ge_ref[sl, 0:HALO, :] = carry_ref[k, sl]
        ge_ref[sl, HALO:HALO + tm, :] = gp[:, cols]
        carry_ref[k, sl] = gp[tm - HALO:tm, cols]
        acc = cb_ref[:, cols]
        for j in range(kconv):
            off = HALO - (kconv - 1) + j
            acc = acc + ge_ref[sl, off:off + tm, :] * cw_ref[j:j + 1, cols]
        act_ref[:, cols] = (acc * jax.nn.sigmoid(acc) * up[:, cols]).astype(BF16)
    o_ref[...] += _dot(act_ref[...], wd_ref[...])

    if final_norm:
        @pl.when(k == nk - 1)
        def _():
            _rmsnorm_rows(o_ref, fg_ref, o_ref, min(256, tm))


def _ffn(x, g, wg, wu, conv_w, conv_b, wd, fg, layer, *, batch, seq, tm, tf, final_norm):
    n, d = x.shape
    f = wg.shape[2]
    nt = seq // tm
    nk = f // tf
    kernel = functools.partial(_ffn_kernel, tm=tm, final_norm=final_norm)
    return pl.pallas_call(
        kernel,
        grid=(batch, nt, nk),
        in_specs=[
            pl.BlockSpec((tm, d), lambda b, t, k: (b * nt + t, 0)),
            pl.BlockSpec((None, 1, d), lambda b, t, k: (layer, 0, 0)),
            pl.BlockSpec((None, d, tf), lambda b, t, k: (layer, 0, k)),
            pl.BlockSpec((None, d, tf), lambda b, t, k: (layer, 0, k)),
            pl.BlockSpec((None, conv_w.shape[1], tf), lambda b, t, k: (layer, 0, k)),
            pl.BlockSpec((None, 1, tf), lambda b, t, k: (layer, 0, k)),
            pl.BlockSpec((None, tf, d), lambda b, t, k: (layer, k, 0)),
            pl.BlockSpec((1, d), lambda b, t, k: (0, 0)),
        ],
        out_specs=pl.BlockSpec((tm, d), lambda b, t, k: (b * nt + t, 0)),
        out_shape=jax.ShapeDtypeStruct((n, d), F32),
        scratch_shapes=[
            pltpu.VMEM((tm, d), BF16),
            pltpu.VMEM((tf // LANES, tm + HALO, LANES), F32),
            pltpu.VMEM((tm, tf), BF16),
            pltpu.VMEM((nk, tf // LANES, HALO, LANES), F32),
        ],
        compiler_params=pltpu.CompilerParams(
            dimension_semantics=("arbitrary", "arbitrary", "arbitrary"),
            vmem_limit_bytes=VMEM_LIMIT),
        name="convffn",
    )(x, g, wg, wu, conv_w, conv_b, wd, fg)


def _pick(n, pref, step=LANES):
    t = min(n, pref)
    while n % t:
        t -= step
        assert t > 0, (n, pref, step)
    return t


def kernel(x, norm1_g, w_in, dn_conv_w, dn_a_log, dn_dt_bias, dn_onorm_g, sg_ln_g, sg_ln_b, sg_w, sg_b, w_branch_a, w_branch_b, w_out, norm2_g, ffn_w_gate, ffn_w_up, ffn_conv_w, ffn_conv_b, ffn_w_down, final_norm_g):
    batch, seq, d = x.shape
    depth = w_in.shape[0]
    heads = dn_a_log.shape[1]
    groups = sg_w.shape[1]
    width = heads * LANES
    d_ff = ffn_w_gate.shape[2]
    n = batch * seq
    assert dn_onorm_g.shape[1] == LANES and sg_w.shape[2] == SG_CHUNK == LANES
    assert groups * LANES == width and d == 2 * width and 2 * heads <= LANES
    assert dn_conv_w.shape[1] - 1 <= HALO and ffn_conv_w.shape[1] - 1 <= HALO
    ba0 = 4 * width
    assert w_in.shape[2] == 6 * width + 2 * d + 2 * heads

    tm_in = _pick(seq, 1024)
    tn_in = _pick(width, 1024)
    tb_dn = _pick(seq, 256)
    tm_mg = _pick(n, 256)
    tm_ff = _pick(seq, 1024)
    tf_ff = _pick(d_ff, 512)

    w_p1 = w_in[:, :, :ba0].astype(BF16)
    w_p2 = w_in[:, :, ba0 + 2 * heads:].astype(BF16)
    w_ba = jnp.pad(w_in[:, :, ba0:ba0 + 2 * heads],
                   ((0, 0), (0, 0), (0, LANES - 2 * heads))).astype(BF16)
    wa16, wb16, wo16 = (w.astype(BF16) for w in (w_branch_a, w_branch_b, w_out))
    wg16, wu16, wd16 = (w.astype(BF16) for w in (ffn_w_gate, ffn_w_up, ffn_w_down))
    norm1 = norm1_g.reshape(depth, 1, d)
    norm2 = norm2_g.reshape(depth, 1, d)
    ln_g = sg_ln_g.reshape(depth, 1, width)
    ln_b = sg_ln_b.reshape(depth, 1, width)
    conv_b = ffn_conv_b.reshape(depth, 1, d_ff)
    fg = final_norm_g.reshape(1, d)

    xf = x.reshape(n, d)
    for l in range(depth):
        proj, ba = _inproj(xf, norm1, w_p1, w_p2, w_ba, dn_conv_w, l,
                           seq=seq, width=width, tm=tm_in, tn=tn_in)

        adt = jnp.zeros((2, LANES), F32)
        adt = adt.at[0, heads:2 * heads].set(dn_a_log[l]).at[1, heads:2 * heads].set(dn_dt_bias[l])
        ya = _deltanet(proj, ba, adt, dn_onorm_g[l].reshape(1, LANES),
                       batch=batch, seq=seq, heads=heads, tb=tb_dn)

        bias_full = jnp.repeat(jnp.transpose(sg_b[l]), LANES, axis=1)
        xf = _merge(ya, proj, xf, ln_g, ln_b, sg_w, bias_full, wa16, wb16, wo16, l,
                    groups=groups, tm=tm_mg)

        xf = _ffn(xf, norm2, wg16, wu16, ffn_conv_w, conv_b, wd16, fg, l,
                  batch=batch, seq=seq, tm=tm_ff, tf=tf_ff, final_norm=(l == depth - 1))
    return xf.reshape(batch, seq, d)
```

```python
import functools

import jax
import jax.numpy as jnp
from jax import lax
from jax.experimental import pallas as pl
from jax.experimental.pallas import tpu as pltpu

EPS = 1e-6
LANES = 128
DN_CHUNK = 64
SG_CHUNK = 128
HALO = 8
VMEM_LIMIT = 60 * 1024 * 1024

F32 = jnp.float32
BF16 = jnp.bfloat16


def _dot(a, b):
    return jnp.dot(a, b, preferred_element_type=F32)


def _dot_nt(a, b):
    return lax.dot_general(a, b, (((1,), (1,)), ((), ())), preferred_element_type=F32)


def _dot_tn(a, b):
    return lax.dot_general(a, b, (((0,), (0,)), ((), ())), preferred_element_type=F32)


def _split3(a):
    hi = a.astype(BF16)
    r = a - hi.astype(F32)
    mid = r.astype(BF16)
    lo = (r - mid.astype(F32)).astype(BF16)
    return hi, mid, lo


def _rmsnorm_rows(x_ref, g_ref, o_ref, rows):
    n = x_ref.shape[0] // rows

    def body(i, _):
        r = pl.ds(pl.multiple_of(i * rows, rows), rows)
        x = x_ref[r, :]
        ms = jnp.mean(x * x, axis=-1, keepdims=True)
        o_ref[r, :] = (x * lax.rsqrt(ms + EPS) * g_ref[...]).astype(o_ref.dtype)
        return 0

    lax.fori_loop(0, n, body, 0)


def _inproj_kernel(x_ref, g_ref, w_ref, wba_ref, cw_ref, o_ref, ba_ref,
                   h_ref, carry_ref, *ge_refs, tm, tn, qkv_tiles, tiles_per_seq, rc):
    i = pl.program_id(0)
    j = pl.program_id(1)
    kconv = cw_ref.shape[0]

    @pl.when(j == 0)
    def _():
        _rmsnorm_rows(x_ref, g_ref, h_ref, min(256, tm))
        ba_ref[...] = _dot(h_ref[...], wba_ref[...])

    @pl.when(j >= qkv_tiles)
    def _():
        o_ref[...] = _dot(h_ref[...], w_ref[...]).astype(o_ref.dtype)

    @pl.when(j < qkv_tiles)
    def _():
        @pl.when(i % tiles_per_seq == 0)
        def _():
            carry_ref[j] = jnp.zeros(carry_ref.shape[1:], F32)

        kind = j // (qkv_tiles // 3)
        qscale = jnp.where(kind == 0, float(LANES) ** -0.5, 1.0)
        nchunks = len(ge_refs)
        nslabs = tn // LANES
        for c in range(nchunks):
            res = _dot(h_ref[c * rc:(c + 1) * rc, :], w_ref[...])
            for hd in range(nslabs):
                cols = slice(hd * LANES, (hd + 1) * LANES)
                ge_refs[c][hd, HALO:HALO + rc, :] = res[:, cols]
                if c + 1 < nchunks:
                    ge_refs[c + 1][hd, 0:HALO, :] = res[rc - HALO:rc, cols]
                else:
                    ge_refs[0][hd, 0:HALO, :] = carry_ref[j, hd]
                    carry_ref[j, hd] = res[rc - HALO:rc, cols]
        for c in range(nchunks):
            rows = slice(c * rc, (c + 1) * rc)
            for hd in range(nslabs):
                cols = slice(hd * LANES, (hd + 1) * LANES)
                acc = None
                for t in range(kconv):
                    off = HALO - (kconv - 1) + t
                    term = ge_refs[c][hd, off:off + rc, :] * cw_ref[t:t + 1, cols]
                    acc = term if acc is None else acc + term
                y = acc * jax.nn.sigmoid(acc)
                nrm = lax.rsqrt(jnp.sum(y * y, axis=-1, keepdims=True) + EPS) * qscale
                o_ref[rows, cols] = (y * jnp.where(kind < 2, nrm, 1.0)).astype(o_ref.dtype)


def _inproj(x, g, w, wba, conv_w, layer, *, seq, width, tm, tn):
    n, d = x.shape
    nc = w.shape[2]
    qkv_tiles = 3 * width // tn
    rc = min(128, tm)
    kernel = functools.partial(_inproj_kernel, tm=tm, tn=tn, qkv_tiles=qkv_tiles,
                               tiles_per_seq=seq // tm, rc=rc)
    return pl.pallas_call(
        kernel,
        grid=(n // tm, nc // tn),
        in_specs=[
            pl.BlockSpec((tm, d), lambda i, j: (i, 0)),
            pl.BlockSpec((None, 1, d), lambda i, j: (layer, 0, 0)),
            pl.BlockSpec((None, d, tn), lambda i, j: (layer, 0, j)),
            pl.BlockSpec((None, d, LANES), lambda i, j: (layer, 0, 0)),
            pl.BlockSpec((None, conv_w.shape[1], tn),
                         lambda i, j: (layer, 0, jnp.minimum(j, qkv_tiles - 1))),
        ],
        out_specs=[
            pl.BlockSpec((tm, tn), lambda i, j: (i, j)),
            pl.BlockSpec((tm, LANES), lambda i, j: (i, 0)),
        ],
        out_shape=[
            jax.ShapeDtypeStruct((n, nc), BF16),
            jax.ShapeDtypeStruct((n, LANES), F32),
        ],
        scratch_shapes=[
            pltpu.VMEM((tm, d), BF16),
            pltpu.VMEM((qkv_tiles, tn // LANES, HALO, LANES), F32),
        ] + [pltpu.VMEM((tn // LANES, rc + HALO, LANES), F32)] * (tm // rc),
        compiler_params=pltpu.CompilerParams(
            dimension_semantics=("arbitrary", "arbitrary"),
            vmem_limit_bytes=VMEM_LIMIT),
        name="inproj",
    )(x, g, w, wba, conv_w)


def _gelu(x):
    return 0.5 * x * (1.0 + lax.erf(x * (0.5 ** 0.5)))


def _mixer_kernel(qkv_ref, z_ref, ba_ref, adt_ref, og_ref, u_ref, v_ref, ga_ref, gb_ref, x_ref,
                  lg_ref, lb_ref, sw_ref, bias_ref, wa_ref, wb_ref, wo_ref, o_ref,
                  gb_s, gt_s, u_s, lhs_s, attn_s, kdec_s, s_ref, ya_s, yb_s, mg_s,
                  *, heads, groups, tb):
    C = DN_CHUNK
    d = o_ref.shape[1]

    @pl.when(pl.program_id(1) == 0)
    def _():
        s_ref[...] = jnp.zeros(s_ref.shape, F32)

    ba = ba_ref[...]
    beta_all = jax.nn.sigmoid(ba)
    xa = ba + adt_ref[1:2, :]
    softplus = jnp.maximum(xa, 0.0) + jnp.log1p(jnp.exp(-jnp.abs(xa)))
    g_all = -jnp.exp(adt_ref[0:1, :]) * softplus
    trow = lax.broadcasted_iota(jnp.int32, (tb, tb), 0)
    tcol = lax.broadcasted_iota(jnp.int32, (tb, tb), 1)
    chunk_tril = jnp.where(((trow // C) == (tcol // C)) & (trow >= tcol), 1.0, 0.0).astype(BF16)
    g3 = _split3(g_all)
    gc_all = _dot(chunk_tril, g3[0]) + (_dot(chunk_tril, g3[1]) + _dot(chunk_tril, g3[2]))
    gt_s[...] = gc_all.T
    lane = lax.broadcasted_iota(jnp.int32, (tb, LANES), 1)
    for h in range(heads):
        bcol = jnp.sum(jnp.where(lane == h, beta_all, 0.0), axis=-1, keepdims=True)
        gcol = jnp.sum(jnp.where(lane == heads + h, gc_all, 0.0), axis=-1, keepdims=True)
        gb_s[h] = jnp.broadcast_to(bcol, (tb, LANES))
        gb_s[heads + h] = jnp.broadcast_to(gcol, (tb, LANES))

    row = lax.broadcasted_iota(jnp.int32, (C, C), 0)
    col = lax.broadcasted_iota(jnp.int32, (C, C), 1)
    incl = row >= col
    strict = row > col
    eye = jnp.where(row == col, 1.0, 0.0).astype(F32)
    sizes = []
    s = 1
    while s < C:
        sizes.append(s)
        s *= 2
    level_masks = [
        ((row // (2 * s)) == (col // (2 * s))) & ((row & s) != 0) & ((col & s) == 0)
        for s in sizes
    ]
    nchunks = tb // C
    problems = [(c, h) for c in range(nchunks) for h in range(heads)]

    lmats = []
    for c, h in problems:
        rows = slice(c * C, (c + 1) * C)
        q = qkv_ref[rows, h * LANES:(h + 1) * LANES].astype(F32)
        k = qkv_ref[rows, (heads + h) * LANES:(heads + h + 1) * LANES].astype(F32)
        beta = gb_s[h, rows, :]
        gcol = gb_s[heads + h, rows, :]
        grow = gt_s[heads + h:heads + h + 1, rows]
        decay = jnp.where(incl, jnp.exp(gcol[:, :C] - grow), 0.0)
        kq = _dot_nt(jnp.concatenate([k * beta, q], axis=0).astype(BF16), k.astype(BF16))
        lmats.append(jnp.where(strict, kq[:C] * decay, 0.0))
        attn_s[h, rows, :] = (kq[C:] * decay).astype(BF16)

    v = _gelu(v_ref[...].astype(F32))
    mu = jnp.mean(v, axis=-1, keepdims=True)
    vc = v - mu
    var = jnp.mean(vc * vc, axis=-1, keepdims=True)
    vg = (vc * lax.rsqrt(var + EPS) * lg_ref[...] + lb_ref[...]).astype(BF16)
    srow = lax.broadcasted_iota(jnp.int32, (SG_CHUNK, SG_CHUNK), 0)
    scol = lax.broadcasted_iota(jnp.int32, (SG_CHUNK, SG_CHUNK), 1)
    for g in range(groups):
        cols = slice(g * LANES, (g + 1) * LANES)
        wg = jnp.where(srow >= scol, sw_ref[g], 0.0).astype(BF16)
        for c in range(tb // SG_CHUNK):
            rows = slice(c * SG_CHUNK, (c + 1) * SG_CHUNK)
            mixed = _dot(wg, vg[rows, cols]) + bias_ref[:, cols]
            yb_s[rows, cols] = (_gelu(u_ref[rows, cols].astype(F32)) * mixed).astype(BF16)

    nb_chunks = 4
    wcol = d // nb_chunks

    def branch_b_chunk(i):
        cols = slice(i * wcol, (i + 1) * wcol)
        mg_s[:, cols] = (jax.nn.sigmoid(gb_ref[:, cols].astype(F32))
                         * _dot(yb_s[...], wb_ref[:, cols]))

    tinvs = [eye - jnp.where(level_masks[0], lm, 0.0) for lm in lmats]
    for m in level_masks[1:]:
        t16 = [t.astype(BF16) for t in tinvs]
        tc = [_dot(t, jnp.where(m, lm, 0.0).astype(BF16)) for t, lm in zip(t16, lmats)]
        tinvs = [t - _dot(x.astype(BF16), t_b) for t, x, t_b in zip(tinvs, tc, t16)]

    for (c, h), tinv in zip(problems, tinvs):
        rows = slice(c * C, (c + 1) * C)
        q = qkv_ref[rows, h * LANES:(h + 1) * LANES].astype(F32)
        k = qkv_ref[rows, (heads + h) * LANES:(heads + h + 1) * LANES].astype(F32)
        v = qkv_ref[rows, (2 * heads + h) * LANES:(2 * heads + h + 1) * LANES].astype(F32)
        beta = gb_s[h, rows, :]
        gcol = gb_s[heads + h, rows, :]
        egc = jnp.exp(gcol)
        rhs = jnp.concatenate([v * beta, k * beta * egc], axis=1).astype(BF16)
        uw = _dot(tinv.astype(BF16), rhs)
        u_s[h, rows, :] = uw[:, :LANES]
        lhs_s[h, 2 * c * C:(2 * c + 1) * C, :] = uw[:, LANES:].astype(BF16)
        lhs_s[h, (2 * c + 1) * C:(2 * c + 2) * C, :] = (q * egc).astype(BF16)
        g_last = gcol[C - 1:C, :]
        kdec_s[h, rows, :] = (k * jnp.exp(g_last - gcol)).astype(BF16)

    for c in range(nchunks):
        rows = slice(c * C, (c + 1) * C)
        s_old = [s_ref[h] for h in range(heads)]
        wq = [_dot(lhs_s[h, 2 * c * C:(2 * c + 2) * C, :], s_old[h].astype(BF16))
              for h in range(heads)]
        for i in range(c * nb_chunks // nchunks, (c + 1) * nb_chunks // nchunks):
            branch_b_chunk(i)
        v16 = [(u_s[h, rows, :] - wq[h][:C]).astype(BF16) for h in range(heads)]
        for h in range(heads):
            g_last = gb_s[heads + h, (c + 1) * C - 1:(c + 1) * C, :]
            s_ref[h] = s_old[h] * jnp.exp(g_last) + _dot_tn(kdec_s[h, rows, :], v16[h])
        for h in range(heads):
            o = wq[h][C:] + _dot(attn_s[h, rows, :], v16[h])
            zz = z_ref[rows, h * LANES:(h + 1) * LANES].astype(F32)
            on = o * lax.rsqrt(jnp.mean(o * o, axis=-1, keepdims=True) + EPS)
            y = on * og_ref[...] * (zz * jax.nn.sigmoid(zz))
            ya_s[rows, h * LANES:(h + 1) * LANES] = y.astype(BF16)

    merged = mg_s[...] + jax.nn.sigmoid(ga_ref[...].astype(F32)) * _dot(ya_s[...], wa_ref[...])
    o_ref[...] = x_ref[...] + _dot(merged.astype(BF16), wo_ref[...])


def _mixer(proj, ba, x, adt, onorm_g, ln_g, ln_b, sg_w, bias_full, wa, wb, wo, layer,
           *, batch, seq, heads, groups, tb):
    n, d = x.shape
    width = heads * LANES
    nt = seq // tb
    gate_block = (6 * width) // d
    const = dict(pipeline_mode=pl.Buffered(1))
    kernel = functools.partial(_mixer_kernel, heads=heads, groups=groups, tb=tb)
    return pl.pallas_call(
        kernel,
        grid=(batch, nt),
        in_specs=[
            pl.BlockSpec((tb, 3 * width), lambda b, t: (b * nt + t, 0)),
            pl.BlockSpec((tb, width), lambda b, t: (b * nt + t, 3)),
            pl.BlockSpec((tb, LANES), lambda b, t: (b * nt + t, 0)),
            pl.BlockSpec(adt.shape, lambda b, t: (0, 0)),
            pl.BlockSpec((1, LANES), lambda b, t: (0, 0)),
            pl.BlockSpec((tb, width), lambda b, t: (b * nt + t, 4)),
            pl.BlockSpec((tb, width), lambda b, t: (b * nt + t, 5)),
            pl.BlockSpec((tb, d), lambda b, t: (b * nt + t, gate_block)),
            pl.BlockSpec((tb, d), lambda b, t: (b * nt + t, gate_block + 1)),
            pl.BlockSpec((tb, d), lambda b, t: (b * nt + t, 0)),
            pl.BlockSpec((None, 1, width), lambda b, t: (layer, 0, 0)),
            pl.BlockSpec((None, 1, width), lambda b, t: (layer, 0, 0)),
            pl.BlockSpec((None,) + sg_w.shape[1:], lambda b, t: (layer, 0, 0, 0)),
            pl.BlockSpec(bias_full.shape, lambda b, t: (0, 0)),
            pl.BlockSpec((None,) + wa.shape[1:], lambda b, t: (layer, 0, 0), **const),
            pl.BlockSpec((None,) + wb.shape[1:], lambda b, t: (layer, 0, 0), **const),
            pl.BlockSpec((None,) + wo.shape[1:], lambda b, t: (layer, 0, 0), **const),
        ],
        out_specs=pl.BlockSpec((tb, d), lambda b, t: (b * nt + t, 0)),
        out_shape=jax.ShapeDtypeStruct((n, d), F32),
        scratch_shapes=[
            pltpu.VMEM((2 * heads, tb, LANES), F32),
            pltpu.VMEM((LANES, tb), F32),
            pltpu.VMEM((heads, tb, LANES), F32),
            pltpu.VMEM((heads, 2 * tb, LANES), BF16),
            pltpu.VMEM((heads, tb, DN_CHUNK), BF16),
            pltpu.VMEM((heads, tb, LANES), BF16),
            pltpu.VMEM((heads, LANES, LANES), F32),
            pltpu.VMEM((tb, width), BF16),
            pltpu.VMEM((tb, width), BF16),
            pltpu.VMEM((tb, d), F32),
        ],
        compiler_params=pltpu.CompilerParams(
            dimension_semantics=("arbitrary", "arbitrary"),
            vmem_limit_bytes=VMEM_LIMIT),
        name="mixers_merge_out",
    )(proj, proj, ba, adt, onorm_g, proj, proj, proj, proj, x, ln_g, ln_b, sg_w, bias_full,
      wa, wb, wo)


def _ffn_kernel(x_ref, g_ref, wg_ref, wu_ref, cw_ref, cb_ref, wd_ref, fg_ref, o_ref,
                h_ref, ge_ref, act_ref, carry_ref, *, tm, final_norm):
    t = pl.program_id(1)
    k = pl.program_id(2)
    nk = pl.num_programs(2)
    kconv = cw_ref.shape[0]

    @pl.when(k == 0)
    def _():
        _rmsnorm_rows(x_ref, g_ref, h_ref, min(256, tm))
        o_ref[...] = x_ref[...]

    @pl.when(t == 0)
    def _():
        carry_ref[k] = jnp.zeros(carry_ref.shape[1:], F32)

    h = h_ref[...]
    gp = _dot(h, wg_ref[...])
    up = _dot(h, wu_ref[...])
    for sl in range(ge_ref.shape[0]):
        cols = slice(sl * LANES, (sl + 1) * LANES)
        ge_ref[sl, 0:HALO, :] = carry_ref[k, sl]
        ge_ref[sl, HALO:HALO + tm, :] = gp[:, cols]
        carry_ref[k, sl] = gp[tm - HALO:tm, cols]
        acc = cb_ref[:, cols]
        for j in range(kconv):
            off = HALO - (kconv - 1) + j
            acc = acc + ge_ref[sl, off:off + tm, :] * cw_ref[j:j + 1, cols]
        act_ref[:, cols] = (acc * jax.nn.sigmoid(acc) * up[:, cols]).astype(BF16)
    o_ref[...] += _dot(act_ref[...], wd_ref[...])

    if final_norm:
        @pl.when(k == nk - 1)
        def _():
            _rmsnorm_rows(o_ref, fg_ref, o_ref, min(256, tm))


def _ffn(x, g, wg, wu, conv_w, conv_b, wd, fg, layer, *, batch, seq, tm, tf, final_norm):
    n, d = x.shape
    f = wg.shape[2]
    nt = seq // tm
    nk = f // tf
    kernel = functools.partial(_ffn_kernel, tm=tm, final_norm=final_norm)
    return pl.pallas_call(
        kernel,
        grid=(batch, nt, nk),
        in_specs=[
            pl.BlockSpec((tm, d), lambda b, t, k: (b * nt + t, 0)),
            pl.BlockSpec((None, 1, d), lambda b, t, k: (layer, 0, 0)),
            pl.BlockSpec((None, d, tf), lambda b, t, k: (layer, 0, k)),
            pl.BlockSpec((None, d, tf), lambda b, t, k: (layer, 0, k)),
            pl.BlockSpec((None, conv_w.shape[1], tf), lambda b, t, k: (layer, 0, k)),
            pl.BlockSpec((None, 1, tf), lambda b, t, k: (layer, 0, k)),
            pl.BlockSpec((None, tf, d), lambda b, t, k: (layer, k, 0)),
            pl.BlockSpec((1, d), lambda b, t, k: (0, 0)),
        ],
        out_specs=pl.BlockSpec((tm, d), lambda b, t, k: (b * nt + t, 0)),
        out_shape=jax.ShapeDtypeStruct((n, d), F32),
        scratch_shapes=[
            pltpu.VMEM((tm, d), BF16),
            pltpu.VMEM((tf // LANES, tm + HALO, LANES), F32),
            pltpu.VMEM((tm, tf), BF16),
            pltpu.VMEM((nk, tf // LANES, HALO, LANES), F32),
        ],
        compiler_params=pltpu.CompilerParams(
            dimension_semantics=("arbitrary", "arbitrary", "arbitrary"),
            vmem_limit_bytes=VMEM_LIMIT),
        name="convffn",
    )(x, g, wg, wu, conv_w, conv_b, wd, fg)


def _pick(n, pref, step=LANES):
    t = min(n, pref)
    while n % t:
        t -= step
        assert t > 0, (n, pref, step)
    return t


def kernel(x, norm1_g, w_in, dn_conv_w, dn_a_log, dn_dt_bias, dn_onorm_g, sg_ln_g, sg_ln_b, sg_w, sg_b, w_branch_a, w_branch_b, w_out, norm2_g, ffn_w_gate, ffn_w_up, ffn_conv_w, ffn_conv_b, ffn_w_down, final_norm_g):
    batch, seq, d = x.shape
    depth = w_in.shape[0]
    heads = dn_a_log.shape[1]
    groups = sg_w.shape[1]
    width = heads * LANES
    d_ff = ffn_w_gate.shape[2]
    n = batch * seq
    assert dn_onorm_g.shape[1] == LANES and sg_w.shape[2] == SG_CHUNK == LANES
    assert groups * LANES == width and d == 2 * width and 2 * heads <= LANES
    assert dn_conv_w.shape[1] - 1 <= HALO and ffn_conv_w.shape[1] - 1 <= HALO
    ba0 = 4 * width
    assert w_in.shape[2] == 6 * width + 2 * d + 2 * heads

    tm_in = _pick(seq, 1024)
    tn_in = _pick(width, 1024)
    tb_mx = _pick(seq, 256, step=SG_CHUNK)
    tm_ff = _pick(seq, 1024)
    tf_ff = _pick(d_ff, 512)

    w_main = jnp.concatenate([w_in[:, :, :ba0], w_in[:, :, ba0 + 2 * heads:]], axis=2).astype(BF16)
    w_ba = jnp.pad(w_in[:, :, ba0:ba0 + 2 * heads],
                   ((0, 0), (0, 0), (0, LANES - 2 * heads))).astype(BF16)
    wa16, wb16, wo16 = (w.astype(BF16) for w in (w_branch_a, w_branch_b, w_out))
    wg16, wu16, wd16 = (w.astype(BF16) for w in (ffn_w_gate, ffn_w_up, ffn_w_down))
    norm1 = norm1_g.reshape(depth, 1, d)
    norm2 = norm2_g.reshape(depth, 1, d)
    ln_g = sg_ln_g.reshape(depth, 1, width)
    ln_b = sg_ln_b.reshape(depth, 1, width)
    conv_b = ffn_conv_b.reshape(depth, 1, d_ff)
    fg = final_norm_g.reshape(1, d)

    xf = x.reshape(n, d)
    for l in range(depth):
        proj, ba = _inproj(xf, norm1, w_main, w_ba, dn_conv_w, l,
                           seq=seq, width=width, tm=tm_in, tn=tn_in)

        adt = jnp.zeros((2, LANES), F32)
        adt = adt.at[0, heads:2 * heads].set(dn_a_log[l]).at[1, heads:2 * heads].set(dn_dt_bias[l])
        bias_full = jnp.repeat(jnp.transpose(sg_b[l]), LANES, axis=1)
        xf = _mixer(proj, ba, xf, adt, dn_onorm_g[l].reshape(1, LANES), ln_g, ln_b, sg_w, bias_full,
                    wa16, wb16, wo16, l, batch=batch, seq=seq, heads=heads, groups=groups, tb=tb_mx)

        xf = _ffn(xf, norm2, wg16, wu16, ffn_conv_w, conv_b, wd16, fg, l,
                  batch=batch, seq=seq, tm=tm_ff, tf=tf_ff, final_norm=(l == depth - 1))
    return xf.reshape(batch, seq, d)
```

```python
import functools

import jax
import jax.numpy as jnp
from jax import lax
from jax.experimental import pallas as pl
from jax.experimental.pallas import tpu as pltpu

EPS = 1e-6
LANES = 128
DN_CHUNK = 64
SG_CHUNK = 128
HALO = 8
VMEM_LIMIT = 60 * 1024 * 1024

F32 = jnp.float32
BF16 = jnp.bfloat16


def _dot(a, b):
    return jnp.dot(a, b, preferred_element_type=F32)


def _dot_nt(a, b):
    return lax.dot_general(a, b, (((1,), (1,)), ((), ())), preferred_element_type=F32)


def _dot_tn(a, b):
    return lax.dot_general(a, b, (((0,), (0,)), ((), ())), preferred_element_type=F32)


def _split3(a):
    hi = a.astype(BF16)
    r = a - hi.astype(F32)
    mid = r.astype(BF16)
    lo = (r - mid.astype(F32)).astype(BF16)
    return hi, mid, lo


def _rmsnorm_rows(x_ref, g_ref, o_ref, rows):
    n = x_ref.shape[0] // rows

    def body(i, _):
        r = pl.ds(pl.multiple_of(i * rows, rows), rows)
        x = x_ref[r, :]
        ms = jnp.mean(x * x, axis=-1, keepdims=True)
        o_ref[r, :] = (x * lax.rsqrt(ms + EPS) * g_ref[...]).astype(o_ref.dtype)
        return 0

    lax.fori_loop(0, n, body, 0)


def _regroup_kernel(w_ref, main_ref, ba_ref, *, ba0, nba):
    main_ref[:, :ba0] = w_ref[:, :ba0].astype(BF16)
    main_ref[:, ba0:] = w_ref[:, ba0 + nba:].astype(BF16)
    ba_ref[...] = jnp.zeros(ba_ref.shape, BF16)
    ba_ref[:, :nba] = w_ref[:, ba0:ba0 + nba].astype(BF16)


def _regroup(w_in, *, ba0, nba, tr):
    depth, d, nc = w_in.shape
    return pl.pallas_call(
        functools.partial(_regroup_kernel, ba0=ba0, nba=nba),
        grid=(depth, d // tr),
        in_specs=[pl.BlockSpec((None, tr, nc), lambda l, i: (l, i, 0))],
        out_specs=[
            pl.BlockSpec((None, tr, nc - nba), lambda l, i: (l, i, 0)),
            pl.BlockSpec((None, tr, LANES), lambda l, i: (l, i, 0)),
        ],
        out_shape=[
            jax.ShapeDtypeStruct((depth, d, nc - nba), BF16),
            jax.ShapeDtypeStruct((depth, d, LANES), BF16),
        ],
        compiler_params=pltpu.CompilerParams(
            dimension_semantics=("arbitrary", "arbitrary"),
            vmem_limit_bytes=VMEM_LIMIT),
        name="regroup_w_in",
    )(w_in)


def _inproj_kernel(x_ref, g_ref, w_ref, wba_ref, cw_ref, o_ref, ba_ref,
                   h_ref, carry_ref, *ge_refs, tm, tn, qkv_tiles, tiles_per_seq, rc):
    i = pl.program_id(0)
    j = pl.program_id(1)
    kconv = cw_ref.shape[0]

    @pl.when(j == 0)
    def _():
        _rmsnorm_rows(x_ref, g_ref, h_ref, min(256, tm))
        ba_ref[...] = _dot(h_ref[...], wba_ref[...])

    @pl.when(j >= qkv_tiles)
    def _():
        o_ref[...] = _dot(h_ref[...], w_ref[...]).astype(o_ref.dtype)

    @pl.when(j < qkv_tiles)
    def _():
        @pl.when(i % tiles_per_seq == 0)
        def _():
            carry_ref[j] = jnp.zeros(carry_ref.shape[1:], F32)

        kind = j // (qkv_tiles // 3)
        qscale = jnp.where(kind == 0, float(LANES) ** -0.5, 1.0)
        nchunks = len(ge_refs)
        nslabs = tn // LANES
        for c in range(nchunks):
            res = _dot(h_ref[c * rc:(c + 1) * rc, :], w_ref[...])
            for hd in range(nslabs):
                cols = slice(hd * LANES, (hd + 1) * LANES)
                ge_refs[c][hd, HALO:HALO + rc, :] = res[:, cols]
                if c + 1 < nchunks:
                    ge_refs[c + 1][hd, 0:HALO, :] = res[rc - HALO:rc, cols]
                else:
                    ge_refs[0][hd, 0:HALO, :] = carry_ref[j, hd]
                    carry_ref[j, hd] = res[rc - HALO:rc, cols]
        for c in range(nchunks):
            rows = slice(c * rc, (c + 1) * rc)
            for hd in range(nslabs):
                cols = slice(hd * LANES, (hd + 1) * LANES)
                acc = None
                for t in range(kconv):
                    off = HALO - (kconv - 1) + t
                    term = ge_refs[c][hd, off:off + rc, :] * cw_ref[t:t + 1, cols]
                    acc = term if acc is None else acc + term
                y = acc * jax.nn.sigmoid(acc)
                nrm = lax.rsqrt(jnp.sum(y * y, axis=-1, keepdims=True) + EPS) * qscale
                o_ref[rows, cols] = (y * jnp.where(kind < 2, nrm, 1.0)).astype(o_ref.dtype)


def _inproj(x, g, w, wba, conv_w, layer, *, seq, width, tm, tn):
    n, d = x.shape
    nc = w.shape[2]
    qkv_tiles = 3 * width // tn
    rc = min(128, tm)
    kernel = functools.partial(_inproj_kernel, tm=tm, tn=tn, qkv_tiles=qkv_tiles,
                               tiles_per_seq=seq // tm, rc=rc)
    return pl.pallas_call(
        kernel,
        grid=(n // tm, nc // tn),
        in_specs=[
            pl.BlockSpec((tm, d), lambda i, j: (i, 0)),
            pl.BlockSpec((None, 1, d), lambda i, j: (layer, 0, 0)),
            pl.BlockSpec((None, d, tn), lambda i, j: (layer, 0, j)),
            pl.BlockSpec((None, d, LANES), lambda i, j: (layer, 0, 0)),
            pl.BlockSpec((None, conv_w.shape[1], tn),
                         lambda i, j: (layer, 0, jnp.minimum(j, qkv_tiles - 1))),
        ],
        out_specs=[
            pl.BlockSpec((tm, tn), lambda i, j: (i, j)),
            pl.BlockSpec((tm, LANES), lambda i, j: (i, 0)),
        ],
        out_shape=[
            jax.ShapeDtypeStruct((n, nc), BF16),
            jax.ShapeDtypeStruct((n, LANES), F32),
        ],
        scratch_shapes=[
            pltpu.VMEM((tm, d), BF16),
            pltpu.VMEM((qkv_tiles, tn // LANES, HALO, LANES), F32),
        ] + [pltpu.VMEM((tn // LANES, rc + HALO, LANES), F32)] * (tm // rc),
        compiler_params=pltpu.CompilerParams(
            dimension_semantics=("arbitrary", "arbitrary"),
            vmem_limit_bytes=VMEM_LIMIT),
        name="inproj",
    )(x, g, w, wba, conv_w)


def _gelu(x):
    return 0.5 * x * (1.0 + lax.erf(x * (0.5 ** 0.5)))


def _mixer_kernel(qkv_ref, z_ref, ba_ref, adt_ref, og_ref, u_ref, v_ref, ga_ref, gb_ref, x_ref,
                  lg_ref, lb_ref, sw_ref, bias_ref, wa_ref, wb_ref, wo_ref, o_ref,
                  gb_s, gt_s, u_s, lhs_s, attn_s, kdec_s, s_ref, ya_s, yb_s, mg_s,
                  *, heads, groups, tb):
    C = DN_CHUNK
    d = o_ref.shape[1]

    @pl.when(pl.program_id(1) == 0)
    def _():
        s_ref[...] = jnp.zeros(s_ref.shape, F32)

    ba = ba_ref[...]
    beta_all = jax.nn.sigmoid(ba)
    xa = ba + adt_ref[1:2, :]
    softplus = jnp.maximum(xa, 0.0) + jnp.log1p(jnp.exp(-jnp.abs(xa)))
    g_all = -jnp.exp(adt_ref[0:1, :]) * softplus
    trow = lax.broadcasted_iota(jnp.int32, (tb, tb), 0)
    tcol = lax.broadcasted_iota(jnp.int32, (tb, tb), 1)
    chunk_tril = jnp.where(((trow // C) == (tcol // C)) & (trow >= tcol), 1.0, 0.0).astype(BF16)
    g3 = _split3(g_all)
    gc_all = _dot(chunk_tril, g3[0]) + (_dot(chunk_tril, g3[1]) + _dot(chunk_tril, g3[2]))
    gt_s[...] = gc_all.T
    lane = lax.broadcasted_iota(jnp.int32, (tb, LANES), 1)
    for h in range(heads):
        bcol = jnp.sum(jnp.where(lane == h, beta_all, 0.0), axis=-1, keepdims=True)
        gcol = jnp.sum(jnp.where(lane == heads + h, gc_all, 0.0), axis=-1, keepdims=True)
        gb_s[h] = jnp.broadcast_to(bcol, (tb, LANES))
        gb_s[heads + h] = jnp.broadcast_to(gcol, (tb, LANES))

    row = lax.broadcasted_iota(jnp.int32, (C, C), 0)
    col = lax.broadcasted_iota(jnp.int32, (C, C), 1)
    incl = row >= col
    strict = row > col
    eye = jnp.where(row == col, 1.0, 0.0).astype(F32)
    sizes = []
    s = 1
    while s < C:
        sizes.append(s)
        s *= 2
    level_masks = [
        ((row // (2 * s)) == (col // (2 * s))) & ((row & s) != 0) & ((col & s) == 0)
        for s in sizes
    ]
    nchunks = tb // C
    problems = [(c, h) for c in range(nchunks) for h in range(heads)]

    lmats = []
    for c, h in problems:
        rows = slice(c * C, (c + 1) * C)
        q = qkv_ref[rows, h * LANES:(h + 1) * LANES].astype(F32)
        k = qkv_ref[rows, (heads + h) * LANES:(heads + h + 1) * LANES].astype(F32)
        beta = gb_s[h, rows, :]
        gcol = gb_s[heads + h, rows, :]
        grow = gt_s[heads + h:heads + h + 1, rows]
        decay = jnp.where(incl, jnp.exp(gcol[:, :C] - grow), 0.0)
        kq = _dot_nt(jnp.concatenate([k * beta, q], axis=0).astype(BF16), k.astype(BF16))
        lmats.append(jnp.where(strict, kq[:C] * decay, 0.0))
        attn_s[h, rows, :] = (kq[C:] * decay).astype(BF16)

    v = _gelu(v_ref[...].astype(F32))
    mu = jnp.mean(v, axis=-1, keepdims=True)
    vc = v - mu
    var = jnp.mean(vc * vc, axis=-1, keepdims=True)
    vg = (vc * lax.rsqrt(var + EPS) * lg_ref[...] + lb_ref[...]).astype(BF16)
    srow = lax.broadcasted_iota(jnp.int32, (SG_CHUNK, SG_CHUNK), 0)
    scol = lax.broadcasted_iota(jnp.int32, (SG_CHUNK, SG_CHUNK), 1)
    for g in range(groups):
        cols = slice(g * LANES, (g + 1) * LANES)
        wg = jnp.where(srow >= scol, sw_ref[g], 0.0).astype(BF16)
        for c in range(tb // SG_CHUNK):
            rows = slice(c * SG_CHUNK, (c + 1) * SG_CHUNK)
            mixed = _dot(wg, vg[rows, cols]) + bias_ref[:, cols]
            yb_s[rows, cols] = (_gelu(u_ref[rows, cols].astype(F32)) * mixed).astype(BF16)

    nb_chunks = 4
    wcol = d // nb_chunks

    def branch_b_chunk(i):
        cols = slice(i * wcol, (i + 1) * wcol)
        mg_s[:, cols] = (jax.nn.sigmoid(gb_ref[:, cols].astype(F32))
                         * _dot(yb_s[...], wb_ref[:, cols]))

    tinvs = [eye - jnp.where(level_masks[0], lm, 0.0) for lm in lmats]
    for m in level_masks[1:]:
        t16 = [t.astype(BF16) for t in tinvs]
        tc = [_dot(t, jnp.where(m, lm, 0.0).astype(BF16)) for t, lm in zip(t16, lmats)]
        tinvs = [t - _dot(x.astype(BF16), t_b) for t, x, t_b in zip(tinvs, tc, t16)]

    for (c, h), tinv in zip(problems, tinvs):
        rows = slice(c * C, (c + 1) * C)
        q = qkv_ref[rows, h * LANES:(h + 1) * LANES].astype(F32)
        k = qkv_ref[rows, (heads + h) * LANES:(heads + h + 1) * LANES].astype(F32)
        v = qkv_ref[rows, (2 * heads + h) * LANES:(2 * heads + h + 1) * LANES].astype(F32)
        beta = gb_s[h, rows, :]
        gcol = gb_s[heads + h, rows, :]
        egc = jnp.exp(gcol)
        rhs = jnp.concatenate([v * beta, k * beta * egc], axis=1).astype(BF16)
        uw = _dot(tinv.astype(BF16), rhs)
        u_s[h, rows, :] = uw[:, :LANES]
        lhs_s[h, 2 * c * C:(2 * c + 1) * C, :] = uw[:, LANES:].astype(BF16)
        lhs_s[h, (2 * c + 1) * C:(2 * c + 2) * C, :] = (q * egc).astype(BF16)
        g_last = gcol[C - 1:C, :]
        kdec_s[h, rows, :] = (k * jnp.exp(g_last - gcol)).astype(BF16)

    for c in range(nchunks):
        rows = slice(c * C, (c + 1) * C)
        s_old = [s_ref[h] for h in range(heads)]
        wq = [_dot(lhs_s[h, 2 * c * C:(2 * c + 2) * C, :], s_old[h].astype(BF16))
              for h in range(heads)]
        for i in range(c * nb_chunks // nchunks, (c + 1) * nb_chunks // nchunks):
            branch_b_chunk(i)
        v16 = [(u_s[h, rows, :] - wq[h][:C]).astype(BF16) for h in range(heads)]
        for h in range(heads):
            g_last = gb_s[heads + h, (c + 1) * C - 1:(c + 1) * C, :]
            s_ref[h] = s_old[h] * jnp.exp(g_last) + _dot_tn(kdec_s[h, rows, :], v16[h])
        for h in range(heads):
            o = wq[h][C:] + _dot(attn_s[h, rows, :], v16[h])
            zz = z_ref[rows, h * LANES:(h + 1) * LANES].astype(F32)
            on = o * lax.rsqrt(jnp.mean(o * o, axis=-1, keepdims=True) + EPS)
            y = on * og_ref[...] * (zz * jax.nn.sigmoid(zz))
            ya_s[rows, h * LANES:(h + 1) * LANES] = y.astype(BF16)

    merged = mg_s[...] + jax.nn.sigmoid(ga_ref[...].astype(F32)) * _dot(ya_s[...], wa_ref[...])
    o_ref[...] = x_ref[...] + _dot(merged.astype(BF16), wo_ref[...])


def _mixer(proj, ba, x, adt, onorm_g, ln_g, ln_b, sg_w, bias_full, wa, wb, wo, layer,
           *, batch, seq, heads, groups, tb):
    n, d = x.shape
    width = heads * LANES
    nt = seq // tb
    gate_block = (6 * width) // d
    const = dict(pipeline_mode=pl.Buffered(1))
    kernel = functools.partial(_mixer_kernel, heads=heads, groups=groups, tb=tb)
    return pl.pallas_call(
        kernel,
        grid=(batch, nt),
        in_specs=[
            pl.BlockSpec((tb, 3 * width), lambda b, t: (b * nt + t, 0)),
            pl.BlockSpec((tb, width), lambda b, t: (b * nt + t, 3)),
            pl.BlockSpec((tb, LANES), lambda b, t: (b * nt + t, 0)),
            pl.BlockSpec(adt.shape, lambda b, t: (0, 0)),
            pl.BlockSpec((1, LANES), lambda b, t: (0, 0)),
            pl.BlockSpec((tb, width), lambda b, t: (b * nt + t, 4)),
            pl.BlockSpec((tb, width), lambda b, t: (b * nt + t, 5)),
            pl.BlockSpec((tb, d), lambda b, t: (b * nt + t, gate_block)),
            pl.BlockSpec((tb, d), lambda b, t: (b * nt + t, gate_block + 1)),
            pl.BlockSpec((tb, d), lambda b, t: (b * nt + t, 0)),
            pl.BlockSpec((None, 1, width), lambda b, t: (layer, 0, 0)),
            pl.BlockSpec((None, 1, width), lambda b, t: (layer, 0, 0)),
            pl.BlockSpec((None,) + sg_w.shape[1:], lambda b, t: (layer, 0, 0, 0)),
            pl.BlockSpec(bias_full.shape, lambda b, t: (0, 0)),
            pl.BlockSpec((None,) + wa.shape[1:], lambda b, t: (layer, 0, 0), **const),
            pl.BlockSpec((None,) + wb.shape[1:], lambda b, t: (layer, 0, 0), **const),
            pl.BlockSpec((None,) + wo.shape[1:], lambda b, t: (layer, 0, 0), **const),
        ],
        out_specs=pl.BlockSpec((tb, d), lambda b, t: (b * nt + t, 0)),
        out_shape=jax.ShapeDtypeStruct((n, d), F32),
        scratch_shapes=[
            pltpu.VMEM((2 * heads, tb, LANES), F32),
            pltpu.VMEM((LANES, tb), F32),
            pltpu.VMEM((heads, tb, LANES), F32),
            pltpu.VMEM((heads, 2 * tb, LANES), BF16),
            pltpu.VMEM((heads, tb, DN_CHUNK), BF16),
            pltpu.VMEM((heads, tb, LANES), BF16),
            pltpu.VMEM((heads, LANES, LANES), F32),
            pltpu.VMEM((tb, width), BF16),
            pltpu.VMEM((tb, width), BF16),
            pltpu.VMEM((tb, d), F32),
        ],
        compiler_params=pltpu.CompilerParams(
            dimension_semantics=("arbitrary", "arbitrary"),
            vmem_limit_bytes=VMEM_LIMIT),
        name="mixers_merge_out",
    )(proj, proj, ba, adt, onorm_g, proj, proj, proj, proj, x, ln_g, ln_b, sg_w, bias_full,
      wa, wb, wo)


def _ffn_kernel(x_ref, g_ref, wg_ref, wu_ref, cw_ref, cb_ref, wd_ref, fg_ref, o_ref,
                h_ref, ge_ref, act_ref, carry_ref, *, tm, final_norm):
    t = pl.program_id(1)
    k = pl.program_id(2)
    nk = pl.num_programs(2)
    kconv = cw_ref.shape[0]

    @pl.when(k == 0)
    def _():
        _rmsnorm_rows(x_ref, g_ref, h_ref, min(256, tm))
        o_ref[...] = x_ref[...]

    @pl.when(t == 0)
    def _():
        carry_ref[k] = jnp.zeros(carry_ref.shape[1:], F32)

    h = h_ref[...]
    gp = _dot(h, wg_ref[...])
    up = _dot(h, wu_ref[...])
    for sl in range(ge_ref.shape[0]):
        cols = slice(sl * LANES, (sl + 1) * LANES)
        ge_ref[sl, 0:HALO, :] = carry_ref[k, sl]
        ge_ref[sl, HALO:HALO + tm, :] = gp[:, cols]
        carry_ref[k, sl] = gp[tm - HALO:tm, cols]
        acc = cb_ref[:, cols]
        for j in range(kconv):
            off = HALO - (kconv - 1) + j
            acc = acc + ge_ref[sl, off:off + tm, :] * cw_ref[j:j + 1, cols]
        act_ref[:, cols] = (acc * jax.nn.sigmoid(acc) * up[:, cols]).astype(BF16)
    o_ref[...] += _dot(act_ref[...], wd_ref[...])

    if final_norm:
        @pl.when(k == nk - 1)
        def _():
            _rmsnorm_rows(o_ref, fg_ref, o_ref, min(256, tm))


def _ffn(x, g, wg, wu, conv_w, conv_b, wd, fg, layer, *, batch, seq, tm, tf, final_norm):
    n, d = x.shape
    f = wg.shape[2]
    nt = seq // tm
    nk = f // tf
    kernel = functools.partial(_ffn_kernel, tm=tm, final_norm=final_norm)
    return pl.pallas_call(
        kernel,
        grid=(batch, nt, nk),
        in_specs=[
            pl.BlockSpec((tm, d), lambda b, t, k: (b * nt + t, 0)),
            pl.BlockSpec((None, 1, d), lambda b, t, k: (layer, 0, 0)),
            pl.BlockSpec((None, d, tf), lambda b, t, k: (layer, 0, k)),
            pl.BlockSpec((None, d, tf), lambda b, t, k: (layer, 0, k)),
            pl.BlockSpec((None, conv_w.shape[1], tf), lambda b, t, k: (layer, 0, k)),
            pl.BlockSpec((None, 1, tf), lambda b, t, k: (layer, 0, k)),
            pl.BlockSpec((None, tf, d), lambda b, t, k: (layer, k, 0)),
            pl.BlockSpec((1, d), lambda b, t, k: (0, 0)),
        ],
        out_specs=pl.BlockSpec((tm, d), lambda b, t, k: (b * nt + t, 0)),
        out_shape=jax.ShapeDtypeStruct((n, d), F32),
        scratch_shapes=[
            pltpu.VMEM((tm, d), BF16),
            pltpu.VMEM((tf // LANES, tm + HALO, LANES), F32),
            pltpu.VMEM((tm, tf), BF16),
            pltpu.VMEM((nk, tf // LANES, HALO, LANES), F32),
        ],
        compiler_params=pltpu.CompilerParams(
            dimension_semantics=("arbitrary", "arbitrary", "arbitrary"),
            vmem_limit_bytes=VMEM_LIMIT),
        name="convffn",
    )(x, g, wg, wu, conv_w, conv_b, wd, fg)


def _pick(n, pref, step=LANES):
    t = min(n, pref)
    while n % t:
        t -= step
        assert t > 0, (n, pref, step)
    return t


def kernel(x, norm1_g, w_in, dn_conv_w, dn_a_log, dn_dt_bias, dn_onorm_g, sg_ln_g, sg_ln_b, sg_w, sg_b, w_branch_a, w_branch_b, w_out, norm2_g, ffn_w_gate, ffn_w_up, ffn_conv_w, ffn_conv_b, ffn_w_down, final_norm_g):
    batch, seq, d = x.shape
    depth = w_in.shape[0]
    heads = dn_a_log.shape[1]
    groups = sg_w.shape[1]
    width = heads * LANES
    d_ff = ffn_w_gate.shape[2]
    n = batch * seq
    assert dn_onorm_g.shape[1] == LANES and sg_w.shape[2] == SG_CHUNK == LANES
    assert groups * LANES == width and d == 2 * width and 2 * heads <= LANES
    assert dn_conv_w.shape[1] - 1 <= HALO and ffn_conv_w.shape[1] - 1 <= HALO
    ba0 = 4 * width
    assert w_in.shape[2] == 6 * width + 2 * d + 2 * heads

    tm_in = _pick(seq, 1024)
    tn_in = _pick(width, 1024)
    tb_mx = _pick(seq, 256, step=SG_CHUNK)
    tm_ff = _pick(seq, 1024)
    tf_ff = _pick(d_ff, 512)

    w_main, w_ba = _regroup(w_in, ba0=ba0, nba=2 * heads, tr=_pick(d, 256, step=16))
    wa16, wb16, wo16 = (w.astype(BF16) for w in (w_branch_a, w_branch_b, w_out))
    wg16, wu16, wd16 = (w.astype(BF16) for w in (ffn_w_gate, ffn_w_up, ffn_w_down))
    norm1 = norm1_g.reshape(depth, 1, d)
    norm2 = norm2_g.reshape(depth, 1, d)
    ln_g = sg_ln_g.reshape(depth, 1, width)
    ln_b = sg_ln_b.reshape(depth, 1, width)
    conv_b = ffn_conv_b.reshape(depth, 1, d_ff)
    fg = final_norm_g.reshape(1, d)

    xf = x.reshape(n, d)
    for l in range(depth):
        proj, ba = _inproj(xf, norm1, w_main, w_ba, dn_conv_w, l,
                           seq=seq, width=width, tm=tm_in, tn=tn_in)

        adt = jnp.zeros((2, LANES), F32)
        adt = adt.at[0, heads:2 * heads].set(dn_a_log[l]).at[1, heads:2 * heads].set(dn_dt_bias[l])
        bias_full = jnp.repeat(jnp.transpose(sg_b[l]), LANES, axis=1)
        xf = _mixer(proj, ba, xf, adt, dn_onorm_g[l].reshape(1, LANES), ln_g, ln_b, sg_w, bias_full,
                    wa16, wb16, wo16, l, batch=batch, seq=seq, heads=heads, groups=groups, tb=tb_mx)

        xf = _ffn(xf, norm2, wg16, wu16, ffn_conv_w, conv_b, wd16, fg, l,
                  batch=batch, seq=seq, tm=tm_ff, tf=tf_ff, final_norm=(l == depth - 1))
    return xf.reshape(batch, seq, d)
```

```python
import functools

import jax
import jax.numpy as jnp
from jax import lax
from jax.experimental import pallas as pl
from jax.experimental.pallas import tpu as pltpu

EPS = 1e-6
LANES = 128
DN_CHUNK = 64
SG_CHUNK = 128
HALO = 8
VMEM_LIMIT = 60 * 1024 * 1024

F32 = jnp.float32
BF16 = jnp.bfloat16


def _dot(a, b):
    return jnp.dot(a, b, preferred_element_type=F32)


def _dot_nt(a, b):
    return lax.dot_general(a, b, (((1,), (1,)), ((), ())), preferred_element_type=F32)


def _dot_tn(a, b):
    return lax.dot_general(a, b, (((0,), (0,)), ((), ())), preferred_element_type=F32)


def _split3(a):
    hi = a.astype(BF16)
    r = a - hi.astype(F32)
    mid = r.astype(BF16)
    lo = (r - mid.astype(F32)).astype(BF16)
    return hi, mid, lo


def _rmsnorm_rows(x_ref, g_ref, o_ref, rows):
    n = x_ref.shape[0] // rows

    def body(i, _):
        r = pl.ds(pl.multiple_of(i * rows, rows), rows)
        x = x_ref[r, :]
        ms = jnp.mean(x * x, axis=-1, keepdims=True)
        o_ref[r, :] = (x * lax.rsqrt(ms + EPS) * g_ref[...]).astype(o_ref.dtype)
        return 0

    lax.fori_loop(0, n, body, 0)


def _inproj_kernel(x_ref, g_ref, w_ref, wba_ref, cw_ref, o_ref, ba_ref,
                   h_ref, carry_ref, *ge_refs, tm, tn, qkv_tiles, tiles_per_seq, rc):
    i = pl.program_id(0)
    j = pl.program_id(1)
    kconv = cw_ref.shape[0]

    @pl.when(j == 0)
    def _():
        _rmsnorm_rows(x_ref, g_ref, h_ref, min(256, tm))
        ba_ref[...] = _dot(h_ref[...], wba_ref[...])

    @pl.when(j >= qkv_tiles)
    def _():
        o_ref[...] = _dot(h_ref[...], w_ref[...]).astype(o_ref.dtype)

    @pl.when(j < qkv_tiles)
    def _():
        @pl.when(i % tiles_per_seq == 0)
        def _():
            carry_ref[j] = jnp.zeros(carry_ref.shape[1:], F32)

        kind = j // (qkv_tiles // 3)
        qscale = jnp.where(kind == 0, float(LANES) ** -0.5, 1.0)
        nchunks = len(ge_refs)
        nslabs = tn // LANES
        for c in range(nchunks):
            res = _dot(h_ref[c * rc:(c + 1) * rc, :], w_ref[...])
            for hd in range(nslabs):
                cols = slice(hd * LANES, (hd + 1) * LANES)
                ge_refs[c][hd, HALO:HALO + rc, :] = res[:, cols]
                if c + 1 < nchunks:
                    ge_refs[c + 1][hd, 0:HALO, :] = res[rc - HALO:rc, cols]
                else:
                    ge_refs[0][hd, 0:HALO, :] = carry_ref[j, hd]
                    carry_ref[j, hd] = res[rc - HALO:rc, cols]
        for c in range(nchunks):
            rows = slice(c * rc, (c + 1) * rc)
            for hd in range(nslabs):
                cols = slice(hd * LANES, (hd + 1) * LANES)
                acc = None
                for t in range(kconv):
                    off = HALO - (kconv - 1) + t
                    term = ge_refs[c][hd, off:off + rc, :] * cw_ref[t:t + 1, cols]
                    acc = term if acc is None else acc + term
                y = acc * jax.nn.sigmoid(acc)
                nrm = lax.rsqrt(jnp.sum(y * y, axis=-1, keepdims=True) + EPS) * qscale
                o_ref[rows, cols] = (y * jnp.where(kind < 2, nrm, 1.0)).astype(o_ref.dtype)


def _inproj(x, g, w, wba, conv_w, layer, *, seq, width, tm, tn):
    n, d = x.shape
    nc = w.shape[2]
    qkv_tiles = 3 * width // tn
    rc = min(128, tm)
    kernel = functools.partial(_inproj_kernel, tm=tm, tn=tn, qkv_tiles=qkv_tiles,
                               tiles_per_seq=seq // tm, rc=rc)
    return pl.pallas_call(
        kernel,
        grid=(n // tm, nc // tn),
        in_specs=[
            pl.BlockSpec((tm, d), lambda i, j: (i, 0)),
            pl.BlockSpec((None, 1, d), lambda i, j: (layer, 0, 0)),
            pl.BlockSpec((None, d, tn), lambda i, j: (layer, 0, j)),
            pl.BlockSpec((None, d, LANES), lambda i, j: (layer, 0, 0)),
            pl.BlockSpec((None, conv_w.shape[1], tn),
                         lambda i, j: (layer, 0, jnp.minimum(j, qkv_tiles - 1))),
        ],
        out_specs=[
            pl.BlockSpec((tm, tn), lambda i, j: (i, j)),
            pl.BlockSpec((tm, LANES), lambda i, j: (i, 0)),
        ],
        out_shape=[
            jax.ShapeDtypeStruct((n, nc), BF16),
            jax.ShapeDtypeStruct((n, LANES), F32),
        ],
        scratch_shapes=[
            pltpu.VMEM((tm, d), BF16),
            pltpu.VMEM((qkv_tiles, tn // LANES, HALO, LANES), F32),
        ] + [pltpu.VMEM((tn // LANES, rc + HALO, LANES), F32)] * (tm // rc),
        compiler_params=pltpu.CompilerParams(
            dimension_semantics=("arbitrary", "arbitrary"),
            vmem_limit_bytes=VMEM_LIMIT),
        name="inproj",
    )(x, g, w, wba, conv_w)


def _gelu(x):
    return 0.5 * x * (1.0 + lax.erf(x * (0.5 ** 0.5)))


def _mixer_kernel(qkv_ref, z_ref, ba_ref, adt_ref, og_ref, u_ref, v_ref, ga_ref, gb_ref, x_ref,
                  lg_ref, lb_ref, sw_ref, bias_ref, wa_ref, wb_ref, wo_ref, o_ref,
                  gb_s, gt_s, u_s, lhs_s, attn_s, kdec_s, s_ref, ya_s, yb_s, mg_s,
                  *, heads, groups, tb):
    C = DN_CHUNK
    d = o_ref.shape[1]

    @pl.when(pl.program_id(1) == 0)
    def _():
        s_ref[...] = jnp.zeros(s_ref.shape, F32)

    ba = ba_ref[...]
    beta_all = jax.nn.sigmoid(ba)
    xa = ba + adt_ref[1:2, :]
    softplus = jnp.maximum(xa, 0.0) + jnp.log1p(jnp.exp(-jnp.abs(xa)))
    g_all = -jnp.exp(adt_ref[0:1, :]) * softplus
    trow = lax.broadcasted_iota(jnp.int32, (tb, tb), 0)
    tcol = lax.broadcasted_iota(jnp.int32, (tb, tb), 1)
    chunk_tril = jnp.where(((trow // C) == (tcol // C)) & (trow >= tcol), 1.0, 0.0).astype(BF16)
    g3 = _split3(g_all)
    gc_all = _dot(chunk_tril, g3[0]) + (_dot(chunk_tril, g3[1]) + _dot(chunk_tril, g3[2]))
    gt_s[...] = gc_all.T
    lane = lax.broadcasted_iota(jnp.int32, (tb, LANES), 1)
    for h in range(heads):
        bcol = jnp.sum(jnp.where(lane == h, beta_all, 0.0), axis=-1, keepdims=True)
        gcol = jnp.sum(jnp.where(lane == heads + h, gc_all, 0.0), axis=-1, keepdims=True)
        gb_s[h] = jnp.broadcast_to(bcol, (tb, LANES))
        gb_s[heads + h] = jnp.broadcast_to(gcol, (tb, LANES))

    P = 2 * C
    row = lax.broadcasted_iota(jnp.int32, (C, P), 0)
    lane_p = lax.broadcasted_iota(jnp.int32, (C, P), 1)
    col = lane_p & (C - 1)
    first = lane_p < C
    incl = row >= col
    strict = row > col
    eye = jnp.where(row == col, 1.0, 0.0).astype(F32)
    sizes = []
    s = 1
    while s < C:
        sizes.append(s)
        s *= 2
    level_masks = [
        ((row // (2 * s)) == (col // (2 * s))) & ((row & s) != 0) & ((col & s) == 0)
        for s in sizes
    ]
    nchunks = tb // C
    pairs = [(m, h) for m in range(tb // P) for h in range(heads)]

    def block_diag16(x):
        return jnp.concatenate([jnp.where(first, x, 0.0).astype(BF16),
                                jnp.where(first, 0.0, x).astype(BF16)], axis=0)

    lmats = []
    for m, h in pairs:
        rows = slice(m * P, (m + 1) * P)
        q = qkv_ref[rows, h * LANES:(h + 1) * LANES].astype(F32)
        k = qkv_ref[rows, (heads + h) * LANES:(heads + h + 1) * LANES].astype(F32)
        beta = gb_s[h, rows, :]
        gcol = gb_s[heads + h, rows, :]
        gsel = jnp.where(first, gcol[:C], gcol[C:])
        grow = gt_s[heads + h:heads + h + 1, rows]
        decay = jnp.where(incl, jnp.exp(gsel - grow), 0.0)
        kq = _dot_nt(jnp.concatenate([k * beta, q], axis=0).astype(BF16), k.astype(BF16))
        kk = jnp.where(first, kq[:C], kq[C:P])
        qk = jnp.where(first, kq[P:P + C], kq[P + C:])
        lmats.append(jnp.where(strict, kk * decay, 0.0))
        attn_s[h, m * C:(m + 1) * C, :] = (qk * decay).astype(BF16)

    v = _gelu(v_ref[...].astype(F32))
    mu = jnp.mean(v, axis=-1, keepdims=True)
    vc = v - mu
    var = jnp.mean(vc * vc, axis=-1, keepdims=True)
    vg = (vc * lax.rsqrt(var + EPS) * lg_ref[...] + lb_ref[...]).astype(BF16)
    srow = lax.broadcasted_iota(jnp.int32, (SG_CHUNK, SG_CHUNK), 0)
    scol = lax.broadcasted_iota(jnp.int32, (SG_CHUNK, SG_CHUNK), 1)
    for g in range(groups):
        cols = slice(g * LANES, (g + 1) * LANES)
        wg = jnp.where(srow >= scol, sw_ref[g], 0.0).astype(BF16)
        for c in range(tb // SG_CHUNK):
            rows = slice(c * SG_CHUNK, (c + 1) * SG_CHUNK)
            mixed = _dot(wg, vg[rows, cols]) + bias_ref[:, cols]
            yb_s[rows, cols] = (_gelu(u_ref[rows, cols].astype(F32)) * mixed).astype(BF16)

    nb_chunks = 4
    wcol = d // nb_chunks

    def branch_b_chunk(i):
        cols = slice(i * wcol, (i + 1) * wcol)
        mg_s[:, cols] = (jax.nn.sigmoid(gb_ref[:, cols].astype(F32))
                         * _dot(yb_s[...], wb_ref[:, cols]))

    tinvs = [eye - jnp.where(level_masks[0], lm, 0.0) for lm in lmats]
    for lm_mask in level_masks[1:]:
        t16 = [t.astype(BF16) for t in tinvs]
        tc = [_dot(t, block_diag16(jnp.where(lm_mask, lm, 0.0))) for t, lm in zip(t16, lmats)]
        tinvs = [t - _dot(x.astype(BF16), block_diag16(t)) for t, x in zip(tinvs, tc)]

    for (m, h), tinv in zip(pairs, tinvs):
        rows = slice(m * P, (m + 1) * P)
        q = qkv_ref[rows, h * LANES:(h + 1) * LANES].astype(F32)
        k = qkv_ref[rows, (heads + h) * LANES:(heads + h + 1) * LANES].astype(F32)
        v = qkv_ref[rows, (2 * heads + h) * LANES:(2 * heads + h + 1) * LANES].astype(F32)
        beta = gb_s[h, rows, :]
        gcol = gb_s[heads + h, rows, :]
        egc = jnp.exp(gcol)
        rhs = jnp.concatenate([v * beta, k * beta * egc], axis=1).astype(BF16)
        uw = _dot(block_diag16(tinv), rhs)
        u_s[h, rows, :] = uw[:, :LANES]
        qe = (q * egc).astype(BF16)
        for j in range(2):
            c = 2 * m + j
            lhs_s[h, 2 * c * C:(2 * c + 1) * C, :] = uw[j * C:(j + 1) * C, LANES:].astype(BF16)
            lhs_s[h, (2 * c + 1) * C:(2 * c + 2) * C, :] = qe[j * C:(j + 1) * C, :]
            g_last = gcol[(j + 1) * C - 1:(j + 1) * C, :]
            kdec_s[h, c * C:(c + 1) * C, :] = (
                k[j * C:(j + 1) * C, :] * jnp.exp(g_last - gcol[j * C:(j + 1) * C, :])).astype(BF16)

    for c in range(nchunks):
        rows = slice(c * C, (c + 1) * C)
        s_old = [s_ref[h] for h in range(heads)]
        wq = [_dot(lhs_s[h, 2 * c * C:(2 * c + 2) * C, :], s_old[h].astype(BF16))
              for h in range(heads)]
        for i in range(c * nb_chunks // nchunks, (c + 1) * nb_chunks // nchunks):
            branch_b_chunk(i)
        v16 = [(u_s[h, rows, :] - wq[h][:C]).astype(BF16) for h in range(heads)]
        for h in range(heads):
            g_last = gb_s[heads + h, (c + 1) * C - 1:(c + 1) * C, :]
            s_ref[h] = s_old[h] * jnp.exp(g_last) + _dot_tn(kdec_s[h, rows, :], v16[h])
        for h in range(heads):
            zero = jnp.zeros((C, LANES), BF16)
            vpad = jnp.concatenate([v16[h], zero] if c % 2 == 0 else [zero, v16[h]], axis=0)
            o = wq[h][C:] + _dot(attn_s[h, (c // 2) * C:(c // 2 + 1) * C, :], vpad)
            zz = z_ref[rows, h * LANES:(h + 1) * LANES].astype(F32)
            on = o * lax.rsqrt(jnp.mean(o * o, axis=-1, keepdims=True) + EPS)
            y = on * og_ref[...] * (zz * jax.nn.sigmoid(zz))
            ya_s[rows, h * LANES:(h + 1) * LANES] = y.astype(BF16)

    merged = mg_s[...] + jax.nn.sigmoid(ga_ref[...].astype(F32)) * _dot(ya_s[...], wa_ref[...])
    o_ref[...] = x_ref[...] + _dot(merged.astype(BF16), wo_ref[...])


def _mixer(proj, ba, x, adt, onorm_g, ln_g, ln_b, sg_w, bias_full, wa, wb, wo, layer,
           *, batch, seq, heads, groups, tb):
    n, d = x.shape
    width = heads * LANES
    nt = seq // tb
    gate_block = (6 * width) // d
    const = dict(pipeline_mode=pl.Buffered(1))
    kernel = functools.partial(_mixer_kernel, heads=heads, groups=groups, tb=tb)
    return pl.pallas_call(
        kernel,
        grid=(batch, nt),
        in_specs=[
            pl.BlockSpec((tb, 3 * width), lambda b, t: (b * nt + t, 0)),
            pl.BlockSpec((tb, width), lambda b, t: (b * nt + t, 3)),
            pl.BlockSpec((tb, LANES), lambda b, t: (b * nt + t, 0)),
            pl.BlockSpec(adt.shape, lambda b, t: (0, 0)),
            pl.BlockSpec((1, LANES), lambda b, t: (0, 0)),
            pl.BlockSpec((tb, width), lambda b, t: (b * nt + t, 4)),
            pl.BlockSpec((tb, width), lambda b, t: (b * nt + t, 5)),
            pl.BlockSpec((tb, d), lambda b, t: (b * nt + t, gate_block)),
            pl.BlockSpec((tb, d), lambda b, t: (b * nt + t, gate_block + 1)),
            pl.BlockSpec((tb, d), lambda b, t: (b * nt + t, 0)),
            pl.BlockSpec((None, 1, width), lambda b, t: (layer, 0, 0)),
            pl.BlockSpec((None, 1, width), lambda b, t: (layer, 0, 0)),
            pl.BlockSpec((None,) + sg_w.shape[1:], lambda b, t: (layer, 0, 0, 0)),
            pl.BlockSpec(bias_full.shape, lambda b, t: (0, 0)),
            pl.BlockSpec((None,) + wa.shape[1:], lambda b, t: (layer, 0, 0), **const),
            pl.BlockSpec((None,) + wb.shape[1:], lambda b, t: (layer, 0, 0), **const),
            pl.BlockSpec((None,) + wo.shape[1:], lambda b, t: (layer, 0, 0), **const),
        ],
        out_specs=pl.BlockSpec((tb, d), lambda b, t: (b * nt + t, 0)),
        out_shape=jax.ShapeDtypeStruct((n, d), F32),
        scratch_shapes=[
            pltpu.VMEM((2 * heads, tb, LANES), F32),
            pltpu.VMEM((LANES, tb), F32),
            pltpu.VMEM((heads, tb, LANES), F32),
            pltpu.VMEM((heads, 2 * tb, LANES), BF16),
            pltpu.VMEM((heads, tb // 2, 2 * DN_CHUNK), BF16),
            pltpu.VMEM((heads, tb, LANES), BF16),
            pltpu.VMEM((heads, LANES, LANES), F32),
            pltpu.VMEM((tb, width), BF16),
            pltpu.VMEM((tb, width), BF16),
            pltpu.VMEM((tb, d), F32),
        ],
        compiler_params=pltpu.CompilerParams(
            dimension_semantics=("arbitrary", "arbitrary"),
            vmem_limit_bytes=VMEM_LIMIT),
        name="mixers_merge_out",
    )(proj, proj, ba, adt, onorm_g, proj, proj, proj, proj, x, ln_g, ln_b, sg_w, bias_full,
      wa, wb, wo)


def _ffn_kernel(x_ref, g_ref, wg_ref, wu_ref, cw_ref, cb_ref, wd_ref, fg_ref, o_ref,
                h_ref, ge_ref, act_ref, carry_ref, *, tm, final_norm):
    t = pl.program_id(1)
    k = pl.program_id(2)
    nk = pl.num_programs(2)
    kconv = cw_ref.shape[0]

    @pl.when(k == 0)
    def _():
        _rmsnorm_rows(x_ref, g_ref, h_ref, min(256, tm))
        o_ref[...] = x_ref[...]

    @pl.when(t == 0)
    def _():
        carry_ref[k] = jnp.zeros(carry_ref.shape[1:], F32)

    h = h_ref[...]
    gp = _dot(h, wg_ref[...])
    up = _dot(h, wu_ref[...])
    for sl in range(ge_ref.shape[0]):
        cols = slice(sl * LANES, (sl + 1) * LANES)
        ge_ref[sl, 0:HALO, :] = carry_ref[k, sl]
        ge_ref[sl, HALO:HALO + tm, :] = gp[:, cols]
        carry_ref[k, sl] = gp[tm - HALO:tm, cols]
        acc = cb_ref[:, cols]
        for j in range(kconv):
            off = HALO - (kconv - 1) + j
            acc = acc + ge_ref[sl, off:off + tm, :] * cw_ref[j:j + 1, cols]
        act_ref[:, cols] = (acc * jax.nn.sigmoid(acc) * up[:, cols]).astype(BF16)
    o_ref[...] += _dot(act_ref[...], wd_ref[...])

    if final_norm:
        @pl.when(k == nk - 1)
        def _():
            _rmsnorm_rows(o_ref, fg_ref, o_ref, min(256, tm))


def _ffn(x, g, wg, wu, conv_w, conv_b, wd, fg, layer, *, batch, seq, tm, tf, final_norm):
    n, d = x.shape
    f = wg.shape[2]
    nt = seq // tm
    nk = f // tf
    kernel = functools.partial(_ffn_kernel, tm=tm, final_norm=final_norm)
    return pl.pallas_call(
        kernel,
        grid=(batch, nt, nk),
        in_specs=[
            pl.BlockSpec((tm, d), lambda b, t, k: (b * nt + t, 0)),
            pl.BlockSpec((None, 1, d), lambda b, t, k: (layer, 0, 0)),
            pl.BlockSpec((None, d, tf), lambda b, t, k: (layer, 0, k)),
            pl.BlockSpec((None, d, tf), lambda b, t, k: (layer, 0, k)),
            pl.BlockSpec((None, conv_w.shape[1], tf), lambda b, t, k: (layer, 0, k)),
            pl.BlockSpec((None, 1, tf), lambda b, t, k: (layer, 0, k)),
            pl.BlockSpec((None, tf, d), lambda b, t, k: (layer, k, 0)),
            pl.BlockSpec((1, d), lambda b, t, k: (0, 0)),
        ],
        out_specs=pl.BlockSpec((tm, d), lambda b, t, k: (b * nt + t, 0)),
        out_shape=jax.ShapeDtypeStruct((n, d), F32),
        scratch_shapes=[
            pltpu.VMEM((tm, d), BF16),
            pltpu.VMEM((tf // LANES, tm + HALO, LANES), F32),
            pltpu.VMEM((tm, tf), BF16),
            pltpu.VMEM((nk, tf // LANES, HALO, LANES), F32),
        ],
        compiler_params=pltpu.CompilerParams(
            dimension_semantics=("arbitrary", "arbitrary", "arbitrary"),
            vmem_limit_bytes=VMEM_LIMIT),
        name="convffn",
    )(x, g, wg, wu, conv_w, conv_b, wd, fg)


def _pick(n, pref, step=LANES):
    t = min(n, pref)
    while n % t:
        t -= step
        assert t > 0, (n, pref, step)
    return t


def kernel(x, norm1_g, w_in, dn_conv_w, dn_a_log, dn_dt_bias, dn_onorm_g, sg_ln_g, sg_ln_b, sg_w, sg_b, w_branch_a, w_branch_b, w_out, norm2_g, ffn_w_gate, ffn_w_up, ffn_conv_w, ffn_conv_b, ffn_w_down, final_norm_g):
    batch, seq, d = x.shape
    depth = w_in.shape[0]
    heads = dn_a_log.shape[1]
    groups = sg_w.shape[1]
    width = heads * LANES
    d_ff = ffn_w_gate.shape[2]
    n = batch * seq
    assert dn_onorm_g.shape[1] == LANES and sg_w.shape[2] == SG_CHUNK == LANES
    assert groups * LANES == width and d == 2 * width and 2 * heads <= LANES
    assert 2 * DN_CHUNK == LANES
    assert dn_conv_w.shape[1] - 1 <= HALO and ffn_conv_w.shape[1] - 1 <= HALO
    ba0 = 4 * width
    assert w_in.shape[2] == 6 * width + 2 * d + 2 * heads

    tm_in = _pick(seq, 1024)
    tn_in = _pick(width, 1024)
    tb_mx = _pick(seq, 256, step=SG_CHUNK)
    tm_ff = _pick(seq, 1024)
    tf_ff = _pick(d_ff, 512)

    w_main = jnp.concatenate([w_in[:, :, :ba0], w_in[:, :, ba0 + 2 * heads:]], axis=2).astype(BF16)
    w_ba = jnp.pad(w_in[:, :, ba0:ba0 + 2 * heads],
                   ((0, 0), (0, 0), (0, LANES - 2 * heads))).astype(BF16)
    wa16, wb16, wo16 = (w.astype(BF16) for w in (w_branch_a, w_branch_b, w_out))
    wg16, wu16, wd16 = (w.astype(BF16) for w in (ffn_w_gate, ffn_w_up, ffn_w_down))
    norm1 = norm1_g.reshape(depth, 1, d)
    norm2 = norm2_g.reshape(depth, 1, d)
    ln_g = sg_ln_g.reshape(depth, 1, width)
    ln_b = sg_ln_b.reshape(depth, 1, width)
    conv_b = ffn_conv_b.reshape(depth, 1, d_ff)
    fg = final_norm_g.reshape(1, d)

    xf = x.reshape(n, d)
    for l in range(depth):
        proj, ba = _inproj(xf, norm1, w_main, w_ba, dn_conv_w, l,
                           seq=seq, width=width, tm=tm_in, tn=tn_in)

        adt = jnp.zeros((2, LANES), F32)
        adt = adt.at[0, heads:2 * heads].set(dn_a_log[l]).at[1, heads:2 * heads].set(dn_dt_bias[l])
        bias_full = jnp.repeat(jnp.transpose(sg_b[l]), LANES, axis=1)
        xf = _mixer(proj, ba, xf, adt, dn_onorm_g[l].reshape(1, LANES), ln_g, ln_b, sg_w, bias_full,
                    wa16, wb16, wo16, l, batch=batch, seq=seq, heads=heads, groups=groups, tb=tb_mx)

        xf = _ffn(xf, norm2, wg16, wu16, ffn_conv_w, conv_b, wd16, fg, l,
                  batch=batch, seq=seq, tm=tm_ff, tf=tf_ff, final_norm=(l == depth - 1))
    return xf.reshape(batch, seq, d)
```

```python
import functools

import jax
import jax.numpy as jnp
from jax import lax
from jax.experimental import pallas as pl
from jax.experimental.pallas import tpu as pltpu

EPS = 1e-6
LANES = 128
DN_CHUNK = 64
SG_CHUNK = 128
HALO = 8
VMEM_LIMIT = 60 * 1024 * 1024

F32 = jnp.float32
BF16 = jnp.bfloat16


def _dot(a, b):
    return jnp.dot(a, b, preferred_element_type=F32)


def _dot_nt(a, b):
    return lax.dot_general(a, b, (((1,), (1,)), ((), ())), preferred_element_type=F32)


def _dot_tn(a, b):
    return lax.dot_general(a, b, (((0,), (0,)), ((), ())), preferred_element_type=F32)


def _split3(a):
    hi = a.astype(BF16)
    r = a - hi.astype(F32)
    mid = r.astype(BF16)
    lo = (r - mid.astype(F32)).astype(BF16)
    return hi, mid, lo


def _rmsnorm_rows(x_ref, g_ref, o_ref, rows):
    n = x_ref.shape[0] // rows

    def body(i, _):
        r = pl.ds(pl.multiple_of(i * rows, rows), rows)
        x = x_ref[r, :]
        ms = jnp.mean(x * x, axis=-1, keepdims=True)
        o_ref[r, :] = (x * lax.rsqrt(ms + EPS) * g_ref[...]).astype(o_ref.dtype)
        return 0

    lax.fori_loop(0, n, body, 0)


def _inproj_kernel(x_ref, g_ref, w_ref, wba_ref, cw_ref, o_ref, ba_ref,
                   h_ref, carry_ref, *ge_refs, tm, tn, qkv_tiles, tiles_per_seq, rc):
    i = pl.program_id(0)
    j = pl.program_id(1)
    kconv = cw_ref.shape[0]

    @pl.when(j == 0)
    def _():
        _rmsnorm_rows(x_ref, g_ref, h_ref, min(256, tm))
        ba_ref[...] = _dot(h_ref[...], wba_ref[...])

    @pl.when(j >= qkv_tiles)
    def _():
        o_ref[...] = _dot(h_ref[...], w_ref[...]).astype(o_ref.dtype)

    @pl.when(j < qkv_tiles)
    def _():
        @pl.when(i % tiles_per_seq == 0)
        def _():
            carry_ref[j] = jnp.zeros(carry_ref.shape[1:], F32)

        kind = j // (qkv_tiles // 3)
        qscale = jnp.where(kind == 0, float(LANES) ** -0.5, 1.0)
        nchunks = len(ge_refs)
        nslabs = tn // LANES
        for c in range(nchunks):
            res = _dot(h_ref[c * rc:(c + 1) * rc, :], w_ref[...])
            for hd in range(nslabs):
                cols = slice(hd * LANES, (hd + 1) * LANES)
                ge_refs[c][hd, HALO:HALO + rc, :] = res[:, cols]
                if c + 1 < nchunks:
                    ge_refs[c + 1][hd, 0:HALO, :] = res[rc - HALO:rc, cols]
                else:
                    ge_refs[0][hd, 0:HALO, :] = carry_ref[j, hd]
                    carry_ref[j, hd] = res[rc - HALO:rc, cols]
        for c in range(nchunks):
            rows = slice(c * rc, (c + 1) * rc)
            for hd in range(nslabs):
                cols = slice(hd * LANES, (hd + 1) * LANES)
                acc = None
                for t in range(kconv):
                    off = HALO - (kconv - 1) + t
                    term = ge_refs[c][hd, off:off + rc, :] * cw_ref[t:t + 1, cols]
                    acc = term if acc is None else acc + term
                y = acc * jax.nn.sigmoid(acc)
                nrm = lax.rsqrt(jnp.sum(y * y, axis=-1, keepdims=True) + EPS) * qscale
                o_ref[rows, cols] = (y * jnp.where(kind < 2, nrm, 1.0)).astype(o_ref.dtype)


def _inproj(x, g, w, wba, conv_w, layer, *, seq, width, tm, tn):
    n, d = x.shape
    nc = w.shape[2]
    qkv_tiles = 3 * width // tn
    rc = min(128, tm)
    kernel = functools.partial(_inproj_kernel, tm=tm, tn=tn, qkv_tiles=qkv_tiles,
                               tiles_per_seq=seq // tm, rc=rc)
    return pl.pallas_call(
        kernel,
        grid=(n // tm, nc // tn),
        in_specs=[
            pl.BlockSpec((tm, d), lambda i, j: (i, 0)),
            pl.BlockSpec((None, 1, d), lambda i, j: (layer, 0, 0)),
            pl.BlockSpec((None, d, tn), lambda i, j: (layer, 0, j)),
            pl.BlockSpec((None, d, LANES), lambda i, j: (layer, 0, 0)),
            pl.BlockSpec((None, conv_w.shape[1], tn),
                         lambda i, j: (layer, 0, jnp.minimum(j, qkv_tiles - 1))),
        ],
        out_specs=[
            pl.BlockSpec((tm, tn), lambda i, j: (i, j)),
            pl.BlockSpec((tm, LANES), lambda i, j: (i, 0)),
        ],
        out_shape=[
            jax.ShapeDtypeStruct((n, nc), BF16),
            jax.ShapeDtypeStruct((n, LANES), F32),
        ],
        scratch_shapes=[
            pltpu.VMEM((tm, d), BF16),
            pltpu.VMEM((qkv_tiles, tn // LANES, HALO, LANES), F32),
        ] + [pltpu.VMEM((tn // LANES, rc + HALO, LANES), F32)] * (tm // rc),
        compiler_params=pltpu.CompilerParams(
            dimension_semantics=("arbitrary", "arbitrary"),
            vmem_limit_bytes=VMEM_LIMIT),
        name="inproj",
    )(x, g, w, wba, conv_w)


def _gelu(x):
    return 0.5 * x * (1.0 + lax.erf(x * (0.5 ** 0.5)))


def _mixer_kernel(qkv_ref, z_ref, ba_ref, adt_ref, og_ref, u_ref, v_ref, ga_ref, gb_ref, x_ref,
                  lg_ref, lb_ref, sw_ref, bias_ref, wa_ref, wb_ref, wo_ref, o_ref,
                  gb_s, gt_s, u_s, lhs_s, attn_s, kdec_s, s_ref, ya_s, yb_s, mg_s, mrg_s,
                  *, heads, groups, tb, steps_per_seq):
    C = DN_CHUNK
    d = o_ref.shape[1]
    step = pl.program_id(0)
    slot = step % 2

    @pl.when(step % steps_per_seq == 0)
    def _():
        s_ref[...] = jnp.zeros(s_ref.shape, F32)

    @pl.when(step == 0)
    def _():
        ya_s[1] = jnp.zeros(ya_s.shape[1:], BF16)
        mg_s[1] = jnp.zeros(mg_s.shape[1:], F32)

    nt_chunks = 8
    ocol = d // nt_chunks

    def tail_merge_chunk(i):
        cols = slice(i * ocol, (i + 1) * ocol)
        mrg_s[:, cols] = (mg_s[1 - slot, :, cols]
                          + jax.nn.sigmoid(ga_ref[:, cols].astype(F32))
                          * _dot(ya_s[1 - slot], wa_ref[:, cols])).astype(BF16)

    def tail_out_chunk(i):
        cols = slice(i * ocol, (i + 1) * ocol)
        o_ref[:, cols] = x_ref[:, cols] + _dot(mrg_s[...], wo_ref[:, cols])

    for i in range(3):
        tail_merge_chunk(i)

    ba = ba_ref[...]
    beta_all = jax.nn.sigmoid(ba)
    xa = ba + adt_ref[1:2, :]
    softplus = jnp.maximum(xa, 0.0) + jnp.log1p(jnp.exp(-jnp.abs(xa)))
    g_all = -jnp.exp(adt_ref[0:1, :]) * softplus
    trow = lax.broadcasted_iota(jnp.int32, (tb, tb), 0)
    tcol = lax.broadcasted_iota(jnp.int32, (tb, tb), 1)
    chunk_tril = jnp.where(((trow // C) == (tcol // C)) & (trow >= tcol), 1.0, 0.0).astype(BF16)
    g3 = _split3(g_all)
    gc_all = _dot(chunk_tril, g3[0]) + (_dot(chunk_tril, g3[1]) + _dot(chunk_tril, g3[2]))
    gt_s[...] = gc_all.T
    lane = lax.broadcasted_iota(jnp.int32, (tb, LANES), 1)
    for h in range(heads):
        bcol = jnp.sum(jnp.where(lane == h, beta_all, 0.0), axis=-1, keepdims=True)
        gcol = jnp.sum(jnp.where(lane == heads + h, gc_all, 0.0), axis=-1, keepdims=True)
        gb_s[h] = jnp.broadcast_to(bcol, (tb, LANES))
        gb_s[heads + h] = jnp.broadcast_to(gcol, (tb, LANES))

    tail_merge_chunk(3)

    P = 2 * C
    row = lax.broadcasted_iota(jnp.int32, (C, P), 0)
    lane_p = lax.broadcasted_iota(jnp.int32, (C, P), 1)
    col = lane_p & (C - 1)
    first = lane_p < C
    incl = row >= col
    strict = row > col
    eye = jnp.where(row == col, 1.0, 0.0).astype(F32)
    sizes = []
    s = 1
    while s < C:
        sizes.append(s)
        s *= 2
    level_masks = [
        ((row // (2 * s)) == (col // (2 * s))) & ((row & s) != 0) & ((col & s) == 0)
        for s in sizes
    ]
    nchunks = tb // C
    pairs = [(m, h) for m in range(tb // P) for h in range(heads)]

    def block_diag16(x):
        return jnp.concatenate([jnp.where(first, x, 0.0).astype(BF16),
                                jnp.where(first, 0.0, x).astype(BF16)], axis=0)

    lmats = []
    for m, h in pairs:
        rows = slice(m * P, (m + 1) * P)
        q = qkv_ref[rows, h * LANES:(h + 1) * LANES].astype(F32)
        k = qkv_ref[rows, (heads + h) * LANES:(heads + h + 1) * LANES].astype(F32)
        beta = gb_s[h, rows, :]
        gcol = gb_s[heads + h, rows, :]
        gsel = jnp.where(first, gcol[:C], gcol[C:])
        grow = gt_s[heads + h:heads + h + 1, rows]
        decay = jnp.where(incl, jnp.exp(gsel - grow), 0.0)
        kq = _dot_nt(jnp.concatenate([k * beta, q], axis=0).astype(BF16), k.astype(BF16))
        kk = jnp.where(first, kq[:C], kq[C:P])
        qk = jnp.where(first, kq[P:P + C], kq[P + C:])
        lmats.append(jnp.where(strict, kk * decay, 0.0))
        attn_s[h, m * C:(m + 1) * C, :] = (qk * decay).astype(BF16)
        if len(lmats) % (len(pairs) // 4) == 0:
            tail_merge_chunk(3 + len(lmats) // (len(pairs) // 4))

    v = _gelu(v_ref[...].astype(F32))
    mu = jnp.mean(v, axis=-1, keepdims=True)
    vc = v - mu
    var = jnp.mean(vc * vc, axis=-1, keepdims=True)
    vg = (vc * lax.rsqrt(var + EPS) * lg_ref[...] + lb_ref[...]).astype(BF16)
    tail_out_chunk(0)
    tail_out_chunk(1)
    srow = lax.broadcasted_iota(jnp.int32, (SG_CHUNK, SG_CHUNK), 0)
    scol = lax.broadcasted_iota(jnp.int32, (SG_CHUNK, SG_CHUNK), 1)
    for g in range(groups):
        cols = slice(g * LANES, (g + 1) * LANES)
        wg = jnp.where(srow >= scol, sw_ref[g], 0.0).astype(BF16)
        for c in range(tb // SG_CHUNK):
            rows = slice(c * SG_CHUNK, (c + 1) * SG_CHUNK)
            mixed = _dot(wg, vg[rows, cols]) + bias_ref[:, cols]
            yb_s[rows, cols] = (_gelu(u_ref[rows, cols].astype(F32)) * mixed).astype(BF16)

    tail_out_chunk(2)
    tail_out_chunk(3)

    nb_chunks = 4
    wcol = d // nb_chunks

    def branch_b_chunk(i):
        cols = slice(i * wcol, (i + 1) * wcol)
        mg_s[slot, :, cols] = (jax.nn.sigmoid(gb_ref[:, cols].astype(F32))
                               * _dot(yb_s[...], wb_ref[:, cols]))

    tinvs = [eye - jnp.where(level_masks[0], lm, 0.0) for lm in lmats]
    for li, lm_mask in enumerate(level_masks[1:]):
        if li < 4:
            tail_out_chunk(4 + li)
        t16 = [t.astype(BF16) for t in tinvs]
        tc = [_dot(t, block_diag16(jnp.where(lm_mask, lm, 0.0))) for t, lm in zip(t16, lmats)]
        tinvs = [t - _dot(x.astype(BF16), block_diag16(t)) for t, x in zip(tinvs, tc)]

    for (m, h), tinv in zip(pairs, tinvs):
        rows = slice(m * P, (m + 1) * P)
        q = qkv_ref[rows, h * LANES:(h + 1) * LANES].astype(F32)
        k = qkv_ref[rows, (heads + h) * LANES:(heads + h + 1) * LANES].astype(F32)
        v = qkv_ref[rows, (2 * heads + h) * LANES:(2 * heads + h + 1) * LANES].astype(F32)
        beta = gb_s[h, rows, :]
        gcol = gb_s[heads + h, rows, :]
        egc = jnp.exp(gcol)
        rhs = jnp.concatenate([v * beta, k * beta * egc], axis=1).astype(BF16)
        uw = _dot(block_diag16(tinv), rhs)
        u_s[h, rows, :] = uw[:, :LANES]
        qe = (q * egc).astype(BF16)
        for j in range(2):
            c = 2 * m + j
            lhs_s[h, 2 * c * C:(2 * c + 1) * C, :] = uw[j * C:(j + 1) * C, LANES:].astype(BF16)
            lhs_s[h, (2 * c + 1) * C:(2 * c + 2) * C, :] = qe[j * C:(j + 1) * C, :]
            g_last = gcol[(j + 1) * C - 1:(j + 1) * C, :]
            kdec_s[h, c * C:(c + 1) * C, :] = (
                k[j * C:(j + 1) * C, :] * jnp.exp(g_last - gcol[j * C:(j + 1) * C, :])).astype(BF16)

    for c in range(nchunks):
        rows = slice(c * C, (c + 1) * C)
        s_old = [s_ref[h] for h in range(heads)]
        wq = [_dot(lhs_s[h, 2 * c * C:(2 * c + 2) * C, :], s_old[h].astype(BF16))
              for h in range(heads)]
        for i in range(c * nb_chunks // nchunks, (c + 1) * nb_chunks // nchunks):
            branch_b_chunk(i)
        v16 = [(u_s[h, rows, :] - wq[h][:C]).astype(BF16) for h in range(heads)]
        for h in range(heads):
            g_last = gb_s[heads + h, (c + 1) * C - 1:(c + 1) * C, :]
            s_ref[h] = s_old[h] * jnp.exp(g_last) + _dot_tn(kdec_s[h, rows, :], v16[h])
        for h in range(heads):
            zero = jnp.zeros((C, LANES), BF16)
            vpad = jnp.concatenate([v16[h], zero] if c % 2 == 0 else [zero, v16[h]], axis=0)
            o = wq[h][C:] + _dot(attn_s[h, (c // 2) * C:(c // 2 + 1) * C, :], vpad)
            zz = z_ref[rows, h * LANES:(h + 1) * LANES].astype(F32)
            on = o * lax.rsqrt(jnp.mean(o * o, axis=-1, keepdims=True) + EPS)
            y = on * og_ref[...] * (zz * jax.nn.sigmoid(zz))
            ya_s[slot, rows, h * LANES:(h + 1) * LANES] = y.astype(BF16)


def _mixer(proj, ba, x, adt, onorm_g, ln_g, ln_b, sg_w, bias_full, wa, wb, wo, layer,
           *, batch, seq, heads, groups, tb):
    n, d = x.shape
    width = heads * LANES
    nt = seq // tb
    last = batch * nt - 1
    gate_block = (6 * width) // d
    const = dict(pipeline_mode=pl.Buffered(1))
    kernel = functools.partial(_mixer_kernel, heads=heads, groups=groups, tb=tb, steps_per_seq=nt)

    def cur(col):
        return lambda s: (jnp.minimum(s, last), col)

    def prev(col):
        return lambda s: (jnp.maximum(s - 1, 0), col)

    return pl.pallas_call(
        kernel,
        grid=(batch * nt + 1,),
        in_specs=[
            pl.BlockSpec((tb, 3 * width), cur(0)),
            pl.BlockSpec((tb, width), cur(3)),
            pl.BlockSpec((tb, LANES), cur(0)),
            pl.BlockSpec(adt.shape, lambda s: (0, 0)),
            pl.BlockSpec((1, LANES), lambda s: (0, 0)),
            pl.BlockSpec((tb, width), cur(4)),
            pl.BlockSpec((tb, width), cur(5)),
            pl.BlockSpec((tb, d), prev(gate_block)),
            pl.BlockSpec((tb, d), cur(gate_block + 1)),
            pl.BlockSpec((tb, d), prev(0)),
            pl.BlockSpec((None, 1, width), lambda s: (layer, 0, 0)),
            pl.BlockSpec((None, 1, width), lambda s: (layer, 0, 0)),
            pl.BlockSpec((None,) + sg_w.shape[1:], lambda s: (layer, 0, 0, 0)),
            pl.BlockSpec(bias_full.shape, lambda s: (0, 0)),
            pl.BlockSpec((None,) + wa.shape[1:], lambda s: (layer, 0, 0), **const),
            pl.BlockSpec((None,) + wb.shape[1:], lambda s: (layer, 0, 0), **const),
            pl.BlockSpec((None,) + wo.shape[1:], lambda s: (layer, 0, 0), **const),
        ],
        out_specs=pl.BlockSpec((tb, d), prev(0)),
        out_shape=jax.ShapeDtypeStruct((n, d), F32),
        scratch_shapes=[
            pltpu.VMEM((2 * heads, tb, LANES), F32),
            pltpu.VMEM((LANES, tb), F32),
            pltpu.VMEM((heads, tb, LANES), F32),
            pltpu.VMEM((heads, 2 * tb, LANES), BF16),
            pltpu.VMEM((heads, tb // 2, 2 * DN_CHUNK), BF16),
            pltpu.VMEM((heads, tb, LANES), BF16),
            pltpu.VMEM((heads, LANES, LANES), F32),
            pltpu.VMEM((2, tb, width), BF16),
            pltpu.VMEM((tb, width), BF16),
            pltpu.VMEM((2, tb, d), F32),
            pltpu.VMEM((tb, d), BF16),
        ],
        compiler_params=pltpu.CompilerParams(
            dimension_semantics=("arbitrary",),
            vmem_limit_bytes=VMEM_LIMIT),
        name="mixers_merge_out",
    )(proj, proj, ba, adt, onorm_g, proj, proj, proj, proj, x, ln_g, ln_b, sg_w, bias_full,
      wa, wb, wo)


def _ffn_kernel(x_ref, g_ref, wg_ref, wu_ref, cw_ref, cb_ref, wd_ref, fg_ref, o_ref,
                h_ref, ge_ref, act_ref, carry_ref, *, tm, final_norm):
    t = pl.program_id(1)
    k = pl.program_id(2)
    nk = pl.num_programs(2)
    kconv = cw_ref.shape[0]

    @pl.when(k == 0)
    def _():
        _rmsnorm_rows(x_ref, g_ref, h_ref, min(256, tm))
        o_ref[...] = x_ref[...]

    @pl.when(t == 0)
    def _():
        carry_ref[k] = jnp.zeros(carry_ref.shape[1:], F32)

    h = h_ref[...]
    gp = _dot(h, wg_ref[...])
    up = _dot(h, wu_ref[...])
    for sl in range(ge_ref.shape[0]):
        cols = slice(sl * LANES, (sl + 1) * LANES)
        ge_ref[sl, 0:HALO, :] = carry_ref[k, sl]
        ge_ref[sl, HALO:HALO + tm, :] = gp[:, cols]
        carry_ref[k, sl] = gp[tm - HALO:tm, cols]
        acc = cb_ref[:, cols]
        for j in range(kconv):
            off = HALO - (kconv - 1) + j
            acc = acc + ge_ref[sl, off:off + tm, :] * cw_ref[j:j + 1, cols]
        act_ref[:, cols] = (acc * jax.nn.sigmoid(acc) * up[:, cols]).astype(BF16)
    o_ref[...] += _dot(act_ref[...], wd_ref[...])

    if final_norm:
        @pl.when(k == nk - 1)
        def _():
            _rmsnorm_rows(o_ref, fg_ref, o_ref, min(256, tm))


def _ffn(x, g, wg, wu, conv_w, conv_b, wd, fg, layer, *, batch, seq, tm, tf, final_norm):
    n, d = x.shape
    f = wg.shape[2]
    nt = seq // tm
    nk = f // tf
    kernel = functools.partial(_ffn_kernel, tm=tm, final_norm=final_norm)
    return pl.pallas_call(
        kernel,
        grid=(batch, nt, nk),
        in_specs=[
            pl.BlockSpec((tm, d), lambda b, t, k: (b * nt + t, 0)),
            pl.BlockSpec((None, 1, d), lambda b, t, k: (layer, 0, 0)),
            pl.BlockSpec((None, d, tf), lambda b, t, k: (layer, 0, k)),
            pl.BlockSpec((None, d, tf), lambda b, t, k: (layer, 0, k)),
            pl.BlockSpec((None, conv_w.shape[1], tf), lambda b, t, k: (layer, 0, k)),
            pl.BlockSpec((None, 1, tf), lambda b, t, k: (layer, 0, k)),
            pl.BlockSpec((None, tf, d), lambda b, t, k: (layer, k, 0)),
            pl.BlockSpec((1, d), lambda b, t, k: (0, 0)),
        ],
        out_specs=pl.BlockSpec((tm, d), lambda b, t, k: (b * nt + t, 0)),
        out_shape=jax.ShapeDtypeStruct((n, d), F32),
        scratch_shapes=[
            pltpu.VMEM((tm, d), BF16),
            pltpu.VMEM((tf // LANES, tm + HALO, LANES), F32),
            pltpu.VMEM((tm, tf), BF16),
            pltpu.VMEM((nk, tf // LANES, HALO, LANES), F32),
        ],
        compiler_params=pltpu.CompilerParams(
            dimension_semantics=("arbitrary", "arbitrary", "arbitrary"),
            vmem_limit_bytes=VMEM_LIMIT),
        name="convffn",
    )(x, g, wg, wu, conv_w, conv_b, wd, fg)


def _pick(n, pref, step=LANES):
    t = min(n, pref)
    while n % t:
        t -= step
        assert t > 0, (n, pref, step)
    return t


def kernel(x, norm1_g, w_in, dn_conv_w, dn_a_log, dn_dt_bias, dn_onorm_g, sg_ln_g, sg_ln_b, sg_w, sg_b, w_branch_a, w_branch_b, w_out, norm2_g, ffn_w_gate, ffn_w_up, ffn_conv_w, ffn_conv_b, ffn_w_down, final_norm_g):
    batch, seq, d = x.shape
    depth = w_in.shape[0]
    heads = dn_a_log.shape[1]
    groups = sg_w.shape[1]
    width = heads * LANES
    d_ff = ffn_w_gate.shape[2]
    n = batch * seq
    assert dn_onorm_g.shape[1] == LANES and sg_w.shape[2] == SG_CHUNK == LANES
    assert groups * LANES == width and d == 2 * width and 2 * heads <= LANES
    assert 2 * DN_CHUNK == LANES
    assert dn_conv_w.shape[1] - 1 <= HALO and ffn_conv_w.shape[1] - 1 <= HALO
    ba0 = 4 * width
    assert w_in.shape[2] == 6 * width + 2 * d + 2 * heads

    tm_in = _pick(seq, 1024)
    tn_in = _pick(width, 1024)
    tb_mx = _pick(seq, 256, step=SG_CHUNK)
    tm_ff = _pick(seq, 1024)
    tf_ff = _pick(d_ff, 512)

    w_main = jnp.concatenate([w_in[:, :, :ba0], w_in[:, :, ba0 + 2 * heads:]], axis=2).astype(BF16)
    w_ba = jnp.pad(w_in[:, :, ba0:ba0 + 2 * heads],
                   ((0, 0), (0, 0), (0, LANES - 2 * heads))).astype(BF16)
    wa16, wb16, wo16 = (w.astype(BF16) for w in (w_branch_a, w_branch_b, w_out))
    wg16, wu16, wd16 = (w.astype(BF16) for w in (ffn_w_gate, ffn_w_up, ffn_w_down))
    norm1 = norm1_g.reshape(depth, 1, d)
    norm2 = norm2_g.reshape(depth, 1, d)
    ln_g = sg_ln_g.reshape(depth, 1, width)
    ln_b = sg_ln_b.reshape(depth, 1, width)
    conv_b = ffn_conv_b.reshape(depth, 1, d_ff)
    fg = final_norm_g.reshape(1, d)

    xf = x.reshape(n, d)
    for l in range(depth):
        proj, ba = _inproj(xf, norm1, w_main, w_ba, dn_conv_w, l,
                           seq=seq, width=width, tm=tm_in, tn=tn_in)

        adt = jnp.zeros((2, LANES), F32)
        adt = adt.at[0, heads:2 * heads].set(dn_a_log[l]).at[1, heads:2 * heads].set(dn_dt_bias[l])
        bias_full = jnp.repeat(jnp.transpose(sg_b[l]), LANES, axis=1)
        xf = _mixer(proj, ba, xf, adt, dn_onorm_g[l].reshape(1, LANES), ln_g, ln_b, sg_w, bias_full,
                    wa16, wb16, wo16, l, batch=batch, seq=seq, heads=heads, groups=groups, tb=tb_mx)

        xf = _ffn(xf, norm2, wg16, wu16, ffn_conv_w, conv_b, wd16, fg, l,
                  batch=batch, seq=seq, tm=tm_ff, tf=tf_ff, final_norm=(l == depth - 1))
    return xf.reshape(batch, seq, d)
```
